```python
import math
import jax
import jax.numpy as jnp
from jax import lax
import numpy as np

D_MODEL = 1024
BATCH = 8
SEQ = 2048
DEPTH = 4

CTX_LEN = 256
GRID_W = 64
RMS_EPS = 1e-6
N_MOD = 9
D_FF = 2816

A_HEADS = 4
A_HEAD_DIM = 64
A_WIDTH = A_HEADS * A_HEAD_DIM
GLR_CHUNK = 16
B_Q_HEADS = 8
B_KV_HEADS = 2
B_GROUP = B_Q_HEADS // B_KV_HEADS
B_HEAD_DIM = 64
B_WIDTH = B_Q_HEADS * B_HEAD_DIM
B_KV_WIDTH = B_KV_HEADS * B_HEAD_DIM
WINDOW = 128
ATTN_BLOCK = 128
ROPE_BASE = 10000.0
ROPE_PAIRS = B_HEAD_DIM // 4
MASK_VALUE = -1e9
C_GROUPS = 16
C_GROUP_CH = 16
C_WIDTH = C_GROUPS * C_GROUP_CH
C_STATE = 64

D_MIX = A_WIDTH + B_WIDTH + C_WIDTH
IN_WIDTHS = (A_WIDTH, A_WIDTH, A_WIDTH, A_WIDTH, A_WIDTH, B_WIDTH, B_KV_WIDTH, B_KV_WIDTH, C_WIDTH)
D_IN = sum(IN_WIDTHS)

kernel_name = 'hybrid_hgrn2_swa_s5_macaron_adaln'


def rms_norm(x, g):
    x32 = x.astype(jnp.float32)
    y = x32 * lax.rsqrt(jnp.mean(x32 * x32, axis=-1, keepdims=True) + RMS_EPS)
    return (y * g.astype(jnp.float32)).astype(x.dtype)


def adaln(h, g, m, i):
    return rms_norm(h, g) * (1 + m[:, :, i, 1]) + m[:, :, i, 0]


def swiglu(h, w1, w2):
    gate, up = jnp.split(h @ w1, 2, axis=-1)
    return (jax.nn.silu(gate) * up) @ w2


def split_columns(p):
    points, acc = [], 0
    for w in IN_WIDTHS[:-1]:
        acc += w
        points.append(acc)
    return jnp.split(p, points, axis=-1)


def axial_rope_tables(length):
    rows = length // GRID_W
    row = jnp.repeat(jnp.arange(rows), GRID_W, total_repeat_length=length)
    col = jnp.tile(jnp.arange(GRID_W), rows)
    inv_freq = ROPE_BASE ** (-jnp.arange(ROPE_PAIRS, dtype=jnp.float32) / ROPE_PAIRS)
    ang = jnp.stack([row.astype(jnp.float32)[:, None] * inv_freq,
                     col.astype(jnp.float32)[:, None] * inv_freq], axis=1)
    return jnp.cos(ang), jnp.sin(ang)


def apply_axial_rope(x, cos, sin):
    b, l, h, d = x.shape
    xr = x.reshape(b, l, h, 2, 2, d // 4)
    x1, x2 = xr[..., 0, :], xr[..., 1, :]
    c = cos[None, :, None].astype(x.dtype)
    s = sin[None, :, None].astype(x.dtype)
    out = jnp.stack([x1 * c - x2 * s, x2 * c + x1 * s], axis=-2)
    return out.reshape(b, l, h, d)


def hgrn2_gates(z, lb):
    z32 = z.astype(jnp.float32)
    f = lb + (1.0 - lb) * jax.nn.sigmoid(z32)
    logf = jnp.log(f)
    k = (1.0 - lb) * jax.nn.sigmoid(-z32)
    return logf, k


def to_heads(a):
    bsz, length, _ = a.shape
    return a.astype(jnp.float32).reshape(bsz, length, A_HEADS, A_HEAD_DIM).transpose(0, 2, 1, 3)


def gated_linear_recurrence(q, k, v, logf, s0):
    bsz, h, length, dk = q.shape
    dv = v.shape[-1]
    n = length // GLR_CHUNK
    q, k, logf = (a.reshape(bsz, h, n, GLR_CHUNK, dk) for a in (q, k, logf))
    v = v.reshape(bsz, h, n, GLR_CHUNK, dv)
    b = jnp.cumsum(logf, axis=3)
    b_last = b[:, :, :, -1]
    u = jnp.einsum('bhnsk,bhnsv->bhnkv', k * jnp.exp(b_last[:, :, :, None] - b), v)

    def step(s, inp):
        dec, un = inp
        return dec[..., None] * s + un, s

    s_fin, s_start = lax.scan(step, s0, (jnp.moveaxis(jnp.exp(b_last), 2, 0), jnp.moveaxis(u, 2, 0)))
    s_start = jnp.moveaxis(s_start, 0, 2)
    o_inter = jnp.einsum('bhntk,bhnkv->bhntv', q * jnp.exp(b), s_start)
    lower = jnp.tril(jnp.ones((GLR_CHUNK, GLR_CHUNK), dtype=bool))[..., None]
    diff = b[:, :, :, :, None, :] - b[:, :, :, None, :, :]
    decay = jnp.where(lower, jnp.exp(jnp.where(lower, diff, 0.0)), 0.0)
    scores = jnp.einsum('bhntk,bhnsk,bhntsk->bhnts', q, k, decay)
    o = o_inter + jnp.einsum('bhnts,bhnsv->bhntv', scores, v)
    return o.reshape(bsz, h, length, dv), s_fin


def hgrn2_direction(lat, ctx_side, reverse):
    if reverse:
        lat = [jnp.flip(a, axis=2) for a in lat]
        ctx_side = [jnp.flip(a, axis=2) for a in ctx_side]
    qc, kc, vc, logfc = ctx_side
    s0 = jnp.zeros(qc.shape[:2] + (A_HEAD_DIM, A_HEAD_DIM), jnp.float32)
    oc, s_ctx = gated_linear_recurrence(qc, kc, vc, logfc, s0)
    o, _ = gated_linear_recurrence(lat[0], lat[1], lat[2], lat[3], s_ctx)
    if reverse:
        o, oc = jnp.flip(o, axis=2), jnp.flip(oc, axis=2)
    return o, oc


def hgrn2_readout(o, g, norm_g):
    o = o * lax.rsqrt(jnp.mean(o * o, axis=-1, keepdims=True) + RMS_EPS)
    bsz, _, length, _ = o.shape
    o = o.transpose(0, 2, 1, 3).reshape(bsz, length, A_WIDTH)
    return (o * norm_g.astype(jnp.float32) * jax.nn.silu(g.astype(jnp.float32))).astype(g.dtype)


def hgrn2_mixer(lat, ctx_side, lb, norm_g):
    q, v, zf, zb, g = lat
    qc, vc, zfc, zbc, gc = ctx_side
    q_h, v_h = to_heads(jax.nn.silu(q)), to_heads(v)
    qc_h, vc_h = to_heads(jax.nn.silu(qc)), to_heads(vc)
    logf_f, k_f = hgrn2_gates(zf, lb[0])
    logfc_f, kc_f = hgrn2_gates(zfc, lb[0])
    logf_b, k_b = hgrn2_gates(zb, lb[1])
    logfc_b, kc_b = hgrn2_gates(zbc, lb[1])
    o_f, oc_f = hgrn2_direction([q_h, to_heads(k_f), v_h, to_heads(logf_f)],
                                [qc_h, to_heads(kc_f), vc_h, to_heads(logfc_f)], False)
    o_b, oc_b = hgrn2_direction([q_h, to_heads(k_b), v_h, to_heads(logf_b)],
                                [qc_h, to_heads(kc_b), vc_h, to_heads(logfc_b)], True)
    return hgrn2_readout(o_f + o_b, g, norm_g), hgrn2_readout(oc_f + oc_b, gc, norm_g)


def sink_softmax(logits, sink):
    s = sink.astype(jnp.float32).reshape((1,) + sink.shape + (1,) * (logits.ndim - 3))
    s = jnp.broadcast_to(s, logits.shape[:-1] + (1,))
    p = jax.nn.softmax(jnp.concatenate([logits, s], axis=-1), axis=-1)
    return p[..., :-1]


def window_gqa_mixer(lat, ctx_side, sink, cos, sin):
    q, k, v = lat
    qc, kc, vc = ctx_side
    bsz, length, _ = q.shape
    lc = qc.shape[1]
    nb = length // ATTN_BLOCK
    band = 3 * ATTN_BLOCK
    scale = B_HEAD_DIM ** -0.5
    sink_g = sink.reshape(B_KV_HEADS, B_GROUP)
    q = apply_axial_rope(q.reshape(bsz, length, B_Q_HEADS, B_HEAD_DIM), cos, sin)
    k = apply_axial_rope(k.reshape(bsz, length, B_KV_HEADS, B_HEAD_DIM), cos, sin)
    v = v.reshape(bsz, length, B_KV_HEADS, B_HEAD_DIM)
    qc = qc.reshape(bsz, lc, B_KV_HEADS, B_GROUP, B_HEAD_DIM)
    kc = kc.reshape(bsz, lc, B_KV_HEADS, B_HEAD_DIM)
    vc = vc.reshape(bsz, lc, B_KV_HEADS, B_HEAD_DIM)
    qb = q.reshape(bsz, nb, ATTN_BLOCK, B_KV_HEADS, B_GROUP, B_HEAD_DIM)
    idx = jnp.arange(nb)[:, None] * ATTN_BLOCK + jnp.arange(band)[None, :]
    pad = ((0, 0), (ATTN_BLOCK, ATTN_BLOCK), (0, 0), (0, 0))
    kb = jnp.pad(k, pad)[:, idx]
    vb = jnp.pad(v, pad)[:, idx]
    s_loc = jnp.einsum('bnqhgd,bnkhd->bhgnqk', qb, kb).astype(jnp.float32) * scale
    s_ctx = jnp.einsum('bnqhgd,bkhd->bhgnqk', qb, kc).astype(jnp.float32) * scale
    t_pos = jnp.arange(nb)[:, None, None] * ATTN_BLOCK + jnp.arange(ATTN_BLOCK)[None, :, None]
    s_pos = jnp.arange(nb)[:, None, None] * ATTN_BLOCK - ATTN_BLOCK + jnp.arange(band)[None, None, :]
    valid = (jnp.abs(t_pos - s_pos) <= WINDOW) & (s_pos >= 0) & (s_pos < length)
    s_loc = jnp.where(valid, s_loc, MASK_VALUE)
    p = sink_softmax(jnp.concatenate([s_loc, s_ctx], axis=-1), sink_g).astype(v.dtype)
    o = (jnp.einsum('bhgnqk,bnkhd->bnqhgd', p[..., :band], vb)
         + jnp.einsum('bhgnqk,bkhd->bnqhgd', p[..., band:], vc))
    o = o.reshape(bsz, length, B_WIDTH)
    sc = jnp.einsum('bqhgd,bkhd->bhgqk', qc, kc).astype(jnp.float32) * scale
    pc = sink_softmax(sc, sink_g).astype(vc.dtype)
    oc = jnp.einsum('bhgqk,bkhd->bqhgd', pc, vc).reshape(bsz, lc, B_WIDTH)
    return o, oc


def zoh_discretise(a_re, a_im, log_dt, b_re, b_im):
    dt = jnp.exp(log_dt)[:, None]
    mag = jnp.exp(a_re * dt)
    ang = a_im * dt
    abar_re, abar_im = mag * jnp.cos(ang), mag * jnp.sin(ang)
    den = a_re * a_re + a_im * a_im
    coef_re = ((abar_re - 1.0) * a_re + abar_im * a_im) / den
    coef_im = (abar_im * a_re - (abar_re - 1.0) * a_im) / den
    bbar_re = coef_re[..., None] * b_re - coef_im[..., None] * b_im
    bbar_im = coef_re[..., None] * b_im + coef_im[..., None] * b_re
    return abar_re, abar_im, bbar_re, bbar_im


def diagonal_scan(abar_re, abar_im, bu_re, bu_im, h0_re, h0_im):
    bu_re = bu_re.at[:, 0].add(abar_re * h0_re - abar_im * h0_im)
    bu_im = bu_im.at[:, 0].add(abar_re * h0_im + abar_im * h0_re)
    a_re = jnp.broadcast_to(abar_re, bu_re.shape)
    a_im = jnp.broadcast_to(abar_im, bu_im.shape)

    def combine(e1, e2):
        a1r, a1i, b1r, b1i = e1
        a2r, a2i, b2r, b2i = e2
        return (a2r * a1r - a2i * a1i, a2r * a1i + a2i * a1r,
                a2r * b1r - a2i * b1i + b2r, a2r * b1i + a2i * b1r + b2i)

    _, _, h_re, h_im = lax.associative_scan(combine, (a_re, a_im, bu_re, bu_im), axis=1)
    return h_re, h_im


def s5_direction(ug, ugc, abar_re, abar_im, bbar_re, bbar_im, c_re, c_im, reverse):
    if reverse:
        ug, ugc = jnp.flip(ug, axis=1), jnp.flip(ugc, axis=1)

    def drive(u):
        return (jnp.einsum('blgc,gpc->blgp', u, bbar_re), jnp.einsum('blgc,gpc->blgp', u, bbar_im))

    def readout(h_re, h_im):
        return jnp.einsum('blgp,gcp->blgc', h_re, c_re) - jnp.einsum('blgp,gcp->blgc', h_im, c_im)

    h0 = jnp.zeros((ugc.shape[0], C_GROUPS, C_STATE), jnp.float32)
    hc_re, hc_im = diagonal_scan(abar_re, abar_im, *drive(ugc), h0, h0)
    h_re, h_im = diagonal_scan(abar_re, abar_im, *drive(ug), hc_re[:, -1], hc_im[:, -1])
    y, yc = readout(h_re, h_im), readout(hc_re, hc_im)
    if reverse:
        y, yc = jnp.flip(y, axis=1), jnp.flip(yc, axis=1)
    return y, yc


def s5_mixer(u, uc, a_re, a_im, log_dt, b_re, b_im, c_re, c_im, d, glu_w, glu_b):
    f32 = jnp.float32
    bsz, length, _ = u.shape
    lc = uc.shape[1]
    u32, uc32 = u.astype(f32), uc.astype(f32)
    ug = u32.reshape(bsz, length, C_GROUPS, C_GROUP_CH)
    ugc = uc32.reshape(bsz, lc, C_GROUPS, C_GROUP_CH)
    y = d.astype(f32) * u32
    yc = d.astype(f32) * uc32
    for k, rev in ((0, False), (1, True)):
        disc = zoh_discretise(a_re[k].astype(f32), a_im[k].astype(f32), log_dt[k].astype(f32),
                              b_re.astype(f32), b_im.astype(f32))
        yk, yck = s5_direction(ug, ugc, *disc, c_re[k].astype(f32), c_im[k].astype(f32), rev)
        y = y + yk.reshape(bsz, length, C_WIDTH)
        yc = yc + yck.reshape(bsz, lc, C_WIDTH)

    def glu(h):
        a, b = jnp.split(jax.nn.gelu(h) @ glu_w.astype(f32) + glu_b.astype(f32), 2, axis=-1)
        return (a * jax.nn.sigmoid(b)).astype(u.dtype)

    return glu(y), glu(yc)


def setup_inputs(seed: int = 0) -> dict:
    key = jax.random.key(seed)
    ks = jax.random.split(key, 25)
    f32 = jnp.float32

    def nrm(k, shape, scale):
        return scale * jax.random.normal(k, shape, f32)

    n_idx = jnp.arange(C_STATE, dtype=f32)
    return {
        'x': nrm(ks[0], (BATCH, SEQ, D_MODEL), 1.0),
        'c': nrm(ks[1], (BATCH, D_MODEL), 1.0),
        'ctx': nrm(ks[2], (BATCH, CTX_LEN, D_MODEL), 1.0),
        'c_ctx': nrm(ks[3], (D_MODEL,), 1.0),
        'ada_w': nrm(ks[4], (DEPTH, D_MODEL, N_MOD * D_MODEL), 0.5 * D_MODEL ** -0.5),
        'ada_b': nrm(ks[5], (DEPTH, N_MOD * D_MODEL), 0.02),
        'norm_g': 1.0 + nrm(ks[6], (DEPTH, 3, D_MODEL), 0.01),
        'ffn_w1': nrm(ks[7], (DEPTH, 2, D_MODEL, 2 * D_FF), D_MODEL ** -0.5),
        'ffn_w2': nrm(ks[8], (DEPTH, 2, D_FF, D_MODEL), D_FF ** -0.5),
        'w_in': nrm(ks[9], (DEPTH, D_MODEL, D_IN), D_MODEL ** -0.5),
        'w_out': nrm(ks[10], (DEPTH, D_MIX, D_MODEL), D_MIX ** -0.5),
        'hgrn_lower_bounds': nrm(ks[11], (DEPTH, 2, A_WIDTH), 0.1),
        'hgrn_norm_g': 1.0 + nrm(ks[12], (DEPTH, A_WIDTH), 0.01),
        'attn_sink': nrm(ks[13], (DEPTH, B_Q_HEADS), 0.1),
        's5_a_re': -0.5 + nrm(ks[14], (DEPTH, 2, C_GROUPS, C_STATE), 0.01),
        's5_a_im': math.pi * n_idx + nrm(ks[15], (DEPTH, 2, C_GROUPS, C_STATE), 0.01),
        's5_log_dt': jax.random.uniform(ks[16], (DEPTH, 2, C_GROUPS), f32, math.log(1e-3), math.log(1e-1)),
        's5_b_re': nrm(ks[17], (DEPTH, C_GROUPS, C_STATE, C_GROUP_CH), (2 * C_GROUP_CH) ** -0.5),
        's5_b_im': nrm(ks[18], (DEPTH, C_GROUPS, C_STATE, C_GROUP_CH), (2 * C_GROUP_CH) ** -0.5),
        's5_c_re': nrm(ks[19], (DEPTH, 2, C_GROUPS, C_GROUP_CH, C_STATE), C_STATE ** -0.5),
        's5_c_im': nrm(ks[20], (DEPTH, 2, C_GROUPS, C_GROUP_CH, C_STATE), C_STATE ** -0.5),
        's5_d': nrm(ks[21], (DEPTH, C_WIDTH), 1.0),
        's5_glu_w': nrm(ks[22], (DEPTH, C_WIDTH, 2 * C_WIDTH), C_WIDTH ** -0.5),
        's5_glu_b': nrm(ks[23], (DEPTH, 2 * C_WIDTH), 0.02),
        'final_norm_g': 1.0 + nrm(ks[24], (D_MODEL,), 0.01),
    }


def reference(x, c, ctx, c_ctx, ada_w, ada_b, norm_g, ffn_w1, ffn_w2, w_in, w_out,
              hgrn_lower_bounds, hgrn_norm_g, attn_sink, s5_a_re, s5_a_im, s5_log_dt,
              s5_b_re, s5_b_im, s5_c_re, s5_c_im, s5_d, s5_glu_w, s5_glu_b, final_norm_g):
    bsz, seq_len, d = x.shape
    cos, sin = axial_rope_tables(seq_len)
    lb_soft = jax.nn.softmax(hgrn_lower_bounds.astype(jnp.float32), axis=0)
    lower_bound = jnp.cumsum(lb_soft, axis=0) - lb_soft[0]
    act_c = jax.nn.silu(c)
    act_cc = jax.nn.silu(c_ctx)
    xc = ctx
    for l in range(DEPTH):
        mod_x = (act_c @ ada_w[l] + ada_b[l]).reshape(bsz, 1, 3, 3, d)
        mod_c = (act_cc @ ada_w[l] + ada_b[l]).reshape(1, 1, 3, 3, d)
        x = x + 0.5 * mod_x[:, :, 0, 2] * swiglu(adaln(x, norm_g[l, 0], mod_x, 0), ffn_w1[l, 0], ffn_w2[l, 0])
        xc = xc + 0.5 * mod_c[:, :, 0, 2] * swiglu(adaln(xc, norm_g[l, 0], mod_c, 0), ffn_w1[l, 0], ffn_w2[l, 0])
        px = split_columns(adaln(x, norm_g[l, 1], mod_x, 1) @ w_in[l])
        pc = split_columns(adaln(xc, norm_g[l, 1], mod_c, 1) @ w_in[l])
        a_x, a_c = hgrn2_mixer(px[0:5], pc[0:5], lower_bound[l], hgrn_norm_g[l])
        b_x, b_c = window_gqa_mixer(px[5:8], pc[5:8], attn_sink[l], cos, sin)
        c_x, c_c = s5_mixer(px[8], pc[8], s5_a_re[l], s5_a_im[l], s5_log_dt[l], s5_b_re[l], s5_b_im[l],
                            s5_c_re[l], s5_c_im[l], s5_d[l], s5_glu_w[l], s5_glu_b[l])
        x = x + mod_x[:, :, 1, 2] * (jnp.concatenate([a_x, b_x, c_x], axis=-1) @ w_out[l])
        if l < DEPTH - 1:
            xc = xc + mod_c[:, :, 1, 2] * (jnp.concatenate([a_c, b_c, c_c], axis=-1) @ w_out[l])
            xc = xc + 0.5 * mod_c[:, :, 2, 2] * swiglu(adaln(xc, norm_g[l, 2], mod_c, 2), ffn_w1[l, 1], ffn_w2[l, 1])
        x = x + 0.5 * mod_x[:, :, 2, 2] * swiglu(adaln(x, norm_g[l, 2], mod_x, 2), ffn_w1[l, 1], ffn_w2[l, 1])
    return rms_norm(x, final_norm_g)
```

```python
import functools
import math

import jax
import jax.numpy as jnp
from jax import lax
from jax.experimental import pallas as pl
from jax.experimental.pallas import tpu as pltpu

F32 = jnp.float32
BF16 = jnp.bfloat16

D_MODEL = 1024
DEPTH = 4
GRID_W = 64
RMS_EPS = 1e-6
N_MOD = 9
D_FF = 2816
A_HEADS = 4
A_HEAD_DIM = 64
A_WIDTH = A_HEADS * A_HEAD_DIM
GLR_CHUNK = 16
B_Q_HEADS = 8
B_KV_HEADS = 2
B_GROUP = B_Q_HEADS // B_KV_HEADS
B_HEAD_DIM = 64
B_WIDTH = B_Q_HEADS * B_HEAD_DIM
B_KV_WIDTH = B_KV_HEADS * B_HEAD_DIM
WINDOW = 128
ATTN_BLOCK = 128
ROPE_BASE = 10000.0
ROPE_PAIRS = B_HEAD_DIM // 4
MASK_VALUE = -1e9
C_GROUPS = 16
C_GROUP_CH = 16
C_WIDTH = C_GROUPS * C_GROUP_CH
C_STATE = 64
C_NSTATE = C_GROUPS * C_STATE
D_MIX = A_WIDTH + B_WIDTH + C_WIDTH
A_IN = 5 * A_WIDTH
D_IN = A_IN + B_WIDTH + 2 * B_KV_WIDTH + C_WIDTH

V7X_VMEM_LIMIT = 56 * 1024 * 1024
ROW_TILE = 256
S5_TILE = 128
NEG_BIG = -1e30


def _cparams(n_axes, vmem=None):
    return pltpu.CompilerParams(dimension_semantics=("arbitrary",) * n_axes, vmem_limit_bytes=vmem)


def _split_hi_lo(x):
    hi = x.astype(BF16)
    lo = (x - hi.astype(F32)).astype(BF16)
    return hi, lo


def _mod_kernel(act_ref, w_ref, b_ref, o_ref):
    a = act_ref[...]
    a = (a * jax.nn.sigmoid(a)).astype(BF16)
    o_ref[0] = jnp.dot(a, w_ref[0].astype(BF16), preferred_element_type=F32) + b_ref[0]


def _modulation(act, ada_w, ada_b):
    depth, d, n = ada_w.shape
    tn = 1024
    return pl.pallas_call(
        _mod_kernel,
        grid=(depth, n // tn),
        in_specs=[pl.BlockSpec((16, d), lambda l, j: (0, 0)),
                  pl.BlockSpec((1, d, tn), lambda l, j: (l, 0, j)),
                  pl.BlockSpec((1, 1, tn), lambda l, j: (l, 0, j))],
        out_specs=pl.BlockSpec((1, 16, tn), lambda l, j: (l, 0, j)),
        out_shape=jax.ShapeDtypeStruct((depth, 16, n), F32),
        compiler_params=_cparams(2),
        name="adaln_mod",
    )(act, ada_w, ada_b.reshape(depth, 1, n))


def _adaln(x, g, shift, scale):
    r = lax.rsqrt(jnp.mean(x * x, axis=-1, keepdims=True) + RMS_EPS)
    return (x * r) * (g * (1.0 + scale)) + shift


def _who_map(nct, nb, comp):
    def index_map(i, b):
        return (jnp.where(i < nct, nb, b) * 3 + comp, 0, 0)
    return index_map


def _ffn_kernel(x_ref, sh_ref, sc_ref, gt_ref, g_ref, w1_ref, w2_ref, o_ref):
    x = x_ref[...]
    h = _adaln(x, g_ref[...], sh_ref[0], sc_ref[0]).astype(BF16)
    gu = jnp.dot(h, w1_ref[...], preferred_element_type=F32)
    gate = gu[:, :D_FF]
    a = (gate * jax.nn.sigmoid(gate) * gu[:, D_FF:]).astype(BF16)
    y = jnp.dot(a, w2_ref[...], preferred_element_type=F32)
    o_ref[...] = x + (0.5 * gt_ref[0]) * y


def _ffn(X, mod3, g, w1, w2, nb, lc):
    tt = X.shape[0]
    d = D_MODEL
    tm = ROW_TILE
    nct = lc // tm
    mspec = lambda comp: pl.BlockSpec((1, 1, d), _who_map(nct, nb, comp))
    const = lambda shape: pl.BlockSpec(shape, lambda i, b: (0,) * len(shape), pipeline_mode=pl.Buffered(1))
    return pl.pallas_call(
        _ffn_kernel,
        grid=(tt // tm, nb),
        in_specs=[pl.BlockSpec((tm, d), lambda i, b: (i, b)), mspec(0), mspec(1), mspec(2),
                  const((1, d)), const((d, 2 * D_FF)), const((D_FF, d))],
        out_specs=pl.BlockSpec((tm, d), lambda i, b: (i, b)),
        out_shape=jax.ShapeDtypeStruct(X.shape, F32),
        input_output_aliases={0: 0},
        compiler_params=_cparams(2, V7X_VMEM_LIMIT),
        name="ffn_half_step",
    )(X, mod3, mod3, mod3, g.reshape(1, d), w1, w2)


def _rope(x, cos, sin, first_half):
    w = x.shape[-1]
    nxt = pltpu.roll(x, w - ROPE_PAIRS, axis=1)
    prv = pltpu.roll(x, ROPE_PAIRS, axis=1)
    return x * cos + jnp.where(first_half, nxt, prv) * sin


def _proj_kernel(x_ref, sh_ref, sc_ref, g_ref, w_ref, cos_ref, sin_ref,
                 pa_ref, pq_ref, pk_ref, pv_ref, pu_ref):
    x = x_ref[...]
    h = _adaln(x, g_ref[...], sh_ref[0], sc_ref[0]).astype(BF16)
    p = jnp.dot(h, w_ref[...], preferred_element_type=F32)
    o = A_IN
    pa_ref[...] = p[:, :o]
    cos = cos_ref[...]
    sin = sin_ref[...]
    lane = lax.broadcasted_iota(jnp.int32, (1, B_WIDTH), 1)
    first_half = (lane % (2 * ROPE_PAIRS)) < ROPE_PAIRS
    cos_q = jnp.concatenate([cos] * (B_WIDTH // 128), axis=1)
    sin_q = jnp.concatenate([sin] * (B_WIDTH // 128), axis=1)
    q = _rope(p[:, o:o + B_WIDTH], cos_q, sin_q, first_half)
    pq_ref[...] = (q * (B_HEAD_DIM ** -0.5)).astype(BF16)
    o += B_WIDTH
    k = _rope(p[:, o:o + B_KV_WIDTH], cos, sin, first_half[:, :B_KV_WIDTH])
    pk_ref[...] = k.astype(BF16)
    o += B_KV_WIDTH
    pv_ref[...] = p[:, o:o + B_KV_WIDTH].astype(BF16)
    o += B_KV_WIDTH
    pu_ref[...] = p[:, o:o + C_WIDTH]


def _proj(X, mod3, g, w_in, cos_t, sin_t, nb, lc):
    tt = X.shape[0]
    d = D_MODEL
    tm = ROW_TILE
    nct = lc // tm
    mspec = lambda comp: pl.BlockSpec((1, 1, d), _who_map(nct, nb, comp))
    const = lambda shape: pl.BlockSpec(shape, lambda i, b: (0,) * len(shape), pipeline_mode=pl.Buffered(1))
    tile = lambda w: pl.BlockSpec((tm, w), lambda i, b: (i, b))
    widths = (A_IN, B_WIDTH, B_KV_WIDTH, B_KV_WIDTH, C_WIDTH)
    dtypes = (F32, BF16, BF16, BF16, F32)
    return pl.pallas_call(
        _proj_kernel,
        grid=(tt // tm, nb),
        in_specs=[tile(d), mspec(0), mspec(1), const((1, d)), const((d, D_IN)),
                  pl.BlockSpec((tm, 128), lambda i, b: (i, 0)), pl.BlockSpec((tm, 128), lambda i, b: (i, 0))],
        out_specs=[tile(w) for w in widths],
        out_shape=[jax.ShapeDtypeStruct((tt, nb * w), dt) for w, dt in zip(widths, dtypes)],
        compiler_params=_cparams(2, V7X_VMEM_LIMIT),
        name="in_proj",
    )(X, mod3, mod3, g.reshape(1, d), w_in, cos_t, sin_t)


def _hgrn_kernel(pa_ref, lb_ref, ng_ref, o_ref,
                 qs_s, k_s, b_s, o_s, p_s, st_s, mask_s, ones_s, tri_s, *, lc):
    tt = pa_ref.shape[0]
    w = A_WIDTH
    ch = GLR_CHUNK
    blk = 128
    n_blk = tt // blk
    n_chunks = tt // ch
    n_ctx_chunks = lc // ch

    row = lax.broadcasted_iota(jnp.int32, (w, w), 0)
    col = lax.broadcasted_iota(jnp.int32, (w, w), 1)
    same_head = jnp.where((row // A_HEAD_DIM) == (col // A_HEAD_DIM), 1.0, 0.0)
    mask_s[...] = same_head
    ones_s[...] = same_head.astype(BF16)

    def seg_sum(x):
        hi, lo = _split_hi_lo(x)
        ones = ones_s[...]
        return (jnp.dot(hi, ones, preferred_element_type=F32)
                + jnp.dot(lo, ones, preferred_element_type=F32))

    def q_body(i, carry):
        rows = pl.ds(pl.multiple_of(i * blk, blk), blk)
        q = pa_ref[rows, 0:w]
        qs_s[rows, :] = q * jax.nn.sigmoid(q)
        return carry
    lax.fori_loop(0, n_blk, q_body, 0)

    t_i = lax.broadcasted_iota(jnp.int32, (ch, w), 0)

    def direction(reverse):
        z_off = (3 if reverse else 2) * w
        lb = lb_ref[1:2, :] if reverse else lb_ref[0:1, :]
        r = lax.broadcasted_iota(jnp.int32, (blk, blk), 0)
        c = lax.broadcasted_iota(jnp.int32, (blk, blk), 1)
        tri = ((r // ch) == (c // ch)) & ((c >= r) if reverse else (c <= r))
        tri_s[...] = jnp.where(tri, 1.0, 0.0).astype(BF16)

        def gate_body(i, carry):
            rows = pl.ds(pl.multiple_of(i * blk, blk), blk)
            z = pa_ref[rows, z_off:z_off + w]
            f = lb + (1.0 - lb) * jax.nn.sigmoid(z)
            logf = jnp.log(f)
            k_s[rows, :] = (1.0 - lb) * jax.nn.sigmoid(-z)
            hi = logf.astype(BF16)
            rem = logf - hi.astype(F32)
            mid = rem.astype(BF16)
            lo = (rem - mid.astype(F32)).astype(BF16)
            t = tri_s[...]
            b_s[rows, :] = (jnp.dot(t, hi, preferred_element_type=F32)
                            + jnp.dot(t, mid, preferred_element_type=F32)
                            + jnp.dot(t, lo, preferred_element_type=F32))
            return carry
        lax.fori_loop(0, n_blk, gate_body, 0)

        st_s[...] = jnp.zeros((w, w), F32)

        def chunk_body(step, carry):
            if reverse:
                cidx = jnp.where(step < n_ctx_chunks, n_ctx_chunks - 1 - step,
                                 n_chunks + n_ctx_chunks - 1 - step)
            else:
                cidx = step
            r0 = pl.multiple_of(cidx * ch, ch)
            rows = pl.ds(r0, ch)
            b = b_s[rows, :]
            q = qs_s[rows, :]
            k = k_s[rows, :]
            v = pa_ref[rows, w:2 * w]
            b_last = b_s[pl.ds(r0 if reverse else r0 + ch - 1, 1), :]
            st = st_s[...]
            qe = (q * jnp.exp(b)).astype(BF16)
            kd = (k * jnp.exp(b_last - b)).astype(BF16)
            o_acc = lax.dot_general(qe, st.astype(BF16), (((1,), (1,)), ((), ())),
                                    preferred_element_type=F32)
            ut = lax.dot_general(v.astype(BF16), kd, (((0,), (0,)), ((), ())),
                                 preferred_element_type=F32)
            st_s[...] = st * jnp.exp(b_last) + ut * mask_s[...]
            for s in range(ch):
                b_row = b_s[pl.ds(r0 + s, 1), :]
                k_row = k_s[pl.ds(r0 + s, 1), :]
                keep = (t_i <= s) if reverse else (t_i >= s)
                e = jnp.exp(jnp.where(keep, b - b_row, NEG_BIG))
                p_s[s * ch:(s + 1) * ch, :] = q * e * k_row
            a = seg_sum(p_s[...])
            for s in range(ch):
                v_row = pa_ref[pl.ds(r0 + s, 1), w:2 * w]
                o_acc = o_acc + a[s * ch:(s + 1) * ch, :] * v_row
            if reverse:
                o_s[rows, :] = o_s[rows, :] + o_acc
            else:
                o_s[rows, :] = o_acc
            return carry
        lax.fori_loop(0, n_chunks, chunk_body, 0)

    direction(False)
    direction(True)

    def out_body(i, carry):
        rows = pl.ds(pl.multiple_of(i * blk, blk), blk)
        o = o_s[rows, :]
        ms = seg_sum(o * o) * (1.0 / A_HEAD_DIM)
        g = pa_ref[rows, 4 * w:5 * w]
        o_ref[rows, :] = o * lax.rsqrt(ms + RMS_EPS) * ng_ref[...] * (g * jax.nn.sigmoid(g))
        return carry
    lax.fori_loop(0, n_blk, out_body, 0)


def _hgrn(pa, lower_bound, norm_g, nb, lc):
    tt = pa.shape[0]
    w = A_WIDTH
    return pl.pallas_call(
        functools.partial(_hgrn_kernel, lc=lc),
        grid=(nb,),
        in_specs=[pl.BlockSpec((tt, A_IN), lambda b: (0, b)),
                  pl.BlockSpec((2, w), lambda b: (0, 0)),
                  pl.BlockSpec((1, w), lambda b: (0, 0))],
        out_specs=pl.BlockSpec((tt, w), lambda b: (0, b)),
        out_shape=jax.ShapeDtypeStruct((tt, nb * w), F32),
        scratch_shapes=[pltpu.VMEM((tt, w), F32), pltpu.VMEM((tt, w), F32), pltpu.VMEM((tt, w), F32),
                        pltpu.VMEM((tt, w), F32), pltpu.VMEM((GLR_CHUNK * GLR_CHUNK, w), F32),
                        pltpu.VMEM((w, w), F32), pltpu.VMEM((w, w), F32), pltpu.VMEM((w, w), BF16),
                        pltpu.VMEM((128, 128), BF16)],
        compiler_params=_cparams(1, V7X_VMEM_LIMIT),
        name="hgrn2_mixer",
    )(pa, lower_bound, norm_g.reshape(1, w))


def _attn_kernel(sink_ref, q_ref, k_ref, v_ref, o_ref, *, lc):
    tt = k_ref.shape[0]
    blk = ATTN_BLOCK
    hd = B_HEAD_DIM
    j = pl.program_id(1)
    n_ctx_blk = lc // blk
    n_blk = tt // blk
    nt = (((1,), (1,)), ((), ()))

    def head_slices(h):
        g = h // B_GROUP
        return slice(h * hd, (h + 1) * hd), slice(g * hd, (g + 1) * hd)

    def finish(h, pieces, vs):
        sink = sink_ref[h]
        m = sink
        for s in pieces:
            m = jnp.maximum(m, jnp.max(s, axis=-1, keepdims=True))
        den = jnp.exp(sink - m)
        acc = None
        for s, vv in zip(pieces, vs):
            p = jnp.exp(s - m)
            den = den + jnp.sum(p, axis=-1, keepdims=True)
            pv = jnp.dot(p.astype(BF16), vv, preferred_element_type=F32)
            acc = pv if acc is None else acc + pv
        return acc / den

    @pl.when(j < n_ctx_blk)
    def _():
        for h in range(B_Q_HEADS):
            qs, ks = head_slices(h)
            q = q_ref[:, qs]
            s = lax.dot_general(q, k_ref[0:lc, ks], nt, preferred_element_type=F32)
            o_ref[:, qs] = finish(h, [s], [v_ref[0:lc, ks]])

    @pl.when(j >= n_ctx_blk)
    def _():
        r_prev = pl.multiple_of((j - 1) * blk, blk)
        r_own = pl.multiple_of(j * blk, blk)
        r_next = pl.multiple_of(jnp.minimum(j + 1, n_blk - 1) * blk, blk)
        row = lax.broadcasted_iota(jnp.int32, (blk, blk), 0)
        col = lax.broadcasted_iota(jnp.int32, (blk, blk), 1)
        keep_prev = col >= row + jnp.where(j > n_ctx_blk, 0, blk)
        keep_next = col <= row - jnp.where(j < n_blk - 1, 0, blk)
        for h in range(B_Q_HEADS):
            qs, ks = head_slices(h)
            q = q_ref[:, qs]
            sp = lax.dot_general(q, k_ref[pl.ds(r_prev, blk), ks], nt, preferred_element_type=F32)
            so = lax.dot_general(q, k_ref[pl.ds(r_own, blk), ks], nt, preferred_element_type=F32)
            sn = lax.dot_general(q, k_ref[pl.ds(r_next, blk), ks], nt, preferred_element_type=F32)
            sc = lax.dot_general(q, k_ref[0:lc, ks], nt, preferred_element_type=F32)
            sp = jnp.where(keep_prev, sp, MASK_VALUE)
            sn = jnp.where(keep_next, sn, MASK_VALUE)
            vs = [v_ref[pl.ds(r_prev, blk), ks], v_ref[pl.ds(r_own, blk), ks],
                  v_ref[pl.ds(r_next, blk), ks], v_ref[0:lc, ks]]
            o_ref[:, qs] = finish(h, [sp, so, sn, sc], vs)


def _attn(pq, pk, pv, sink, nb, lc):
    tt = pq.shape[0]
    blk = ATTN_BLOCK
    return pl.pallas_call(
        functools.partial(_attn_kernel, lc=lc),
        grid=(nb, tt // blk),
        in_specs=[pl.BlockSpec(memory_space=pltpu.SMEM),
                  pl.BlockSpec((blk, B_WIDTH), lambda b, j: (j, b)),
                  pl.BlockSpec((tt, B_KV_WIDTH), lambda b, j: (0, b)),
                  pl.BlockSpec((tt, B_KV_WIDTH), lambda b, j: (0, b))],
        out_specs=pl.BlockSpec((blk, B_WIDTH), lambda b, j: (j, b)),
        out_shape=jax.ShapeDtypeStruct((tt, nb * B_WIDTH), F32),
        compiler_params=_cparams(2),
        name="window_gqa",
    )(sink, pq, pk, pv)


def _s5_tile_index(d, i, n_ctx, n_all):
    rev = jnp.where(i < n_ctx, n_ctx - 1 - i, n_all + n_ctx - 1 - i)
    return jnp.where(d == 0, i, rev)


def _s5_kernel(u_ref, a_ref, wb_ref, wc_ref, y_ref, bu_s, h_s, *, nb):
    d = pl.program_id(0)
    i = pl.program_id(1)
    n = C_NSTATE
    steps = u_ref.shape[0] // nb

    @pl.when(i == 0)
    def _():
        h_s[...] = jnp.zeros(h_s.shape, F32)

    bu_s[...] = jnp.dot(u_ref[...].astype(BF16), wb_ref[0], preferred_element_type=F32)
    a_re = jnp.broadcast_to(a_ref[0, 0:1, :], (nb, n))
    a_im = jnp.broadcast_to(a_ref[0, 1:2, :], (nb, n))

    def step(t, carry):
        h_re, h_im = carry
        tt = jnp.where(d == 0, t, steps - 1 - t)
        rows = pl.ds(pl.multiple_of(tt * nb, nb), nb)
        n_re = a_re * h_re - a_im * h_im + bu_s[rows, 0:n]
        n_im = a_re * h_im + a_im * h_re + bu_s[rows, n:2 * n]
        bu_s[rows, 0:n] = n_re
        bu_s[rows, n:2 * n] = n_im
        return n_re, n_im

    h_re, h_im = lax.fori_loop(0, steps, step, (h_s[:, 0:n], h_s[:, n:2 * n]), unroll=4)
    h_s[:, 0:n] = h_re
    h_s[:, n:2 * n] = h_im
    y_ref[0] = jnp.dot(bu_s[...].astype(BF16), wc_ref[0], preferred_element_type=F32)


def _s5_scan(u_flat, a, wb, wc, nb, lc):
    rows = u_flat.shape[0]
    tile_rows = S5_TILE * nb
    n_all = rows // tile_rows
    n_ctx = (lc * nb) // tile_rows
    tile = lambda d, i: _s5_tile_index(d, i, n_ctx, n_all)
    return pl.pallas_call(
        functools.partial(_s5_kernel, nb=nb),
        grid=(2, n_all),
        in_specs=[pl.BlockSpec((tile_rows, C_WIDTH), lambda d, i: (tile(d, i), 0)),
                  pl.BlockSpec((1, 2, C_NSTATE), lambda d, i: (d, 0, 0)),
                  pl.BlockSpec((1, C_WIDTH, 2 * C_NSTATE), lambda d, i: (d, 0, 0)),
                  pl.BlockSpec((1, 2 * C_NSTATE, C_WIDTH), lambda d, i: (d, 0, 0))],
        out_specs=pl.BlockSpec((1, tile_rows, C_WIDTH), lambda d, i: (d, tile(d, i), 0)),
        out_shape=jax.ShapeDtypeStruct((2, rows, C_WIDTH), F32),
        scratch_shapes=[pltpu.VMEM((tile_rows, 2 * C_NSTATE), F32), pltpu.VMEM((nb, 2 * C_NSTATE), F32)],
        compiler_params=_cparams(2, V7X_VMEM_LIMIT),
        name="s5_scan",
    )(u_flat, a, wb, wc)


def _s5_params(a_re, a_im, log_dt, b_re, b_im, c_re, c_im):
    eye = jnp.eye(C_GROUPS, dtype=F32)
    a_out, wb_out, wc_out = [], [], []
    for k in range(2):
        dt = jnp.exp(log_dt[k])[:, None]
        mag = jnp.exp(a_re[k] * dt)
        ang = a_im[k] * dt
        abar_re, abar_im = mag * jnp.cos(ang), mag * jnp.sin(ang)
        den = a_re[k] * a_re[k] + a_im[k] * a_im[k]
        coef_re = ((abar_re - 1.0) * a_re[k] + abar_im * a_im[k]) / den
        coef_im = (abar_im * a_re[k] - (abar_re - 1.0) * a_im[k]) / den
        bbar_re = coef_re[..., None] * b_re - coef_im[..., None] * b_im
        bbar_im = coef_re[..., None] * b_im + coef_im[..., None] * b_re
        a_out.append(jnp.stack([abar_re.reshape(-1), abar_im.reshape(-1)]))
        drive = lambda bb: jnp.einsum('gpc,gh->gchp', bb, eye).reshape(C_WIDTH, C_NSTATE)
        read = lambda cc: jnp.einsum('gcp,gh->gphc', cc, eye).reshape(C_NSTATE, C_WIDTH)
        wb_out.append(jnp.concatenate([drive(bbar_re), drive(bbar_im)], axis=1))
        wc_out.append(jnp.concatenate([read(c_re[k]), -read(c_im[k])], axis=0))
    return jnp.stack(a_out), jnp.stack(wb_out).astype(BF16), jnp.stack(wc_out).astype(BF16)


def _out_kernel(x_ref, gt_ref, a_ref, b_ref, u_ref, yf_ref, yb_ref, d_ref, gw_ref, gb_ref, wo_ref, o_ref):
    y = d_ref[...] * u_ref[...] + yf_ref[0] + yb_ref[0]
    gelu = 0.5 * y * (1.0 + jnp.tanh(math.sqrt(2.0 / math.pi) * (y + 0.044715 * (y * y * y))))
    hg = jnp.dot(gelu.astype(BF16), gw_ref[...], preferred_element_type=F32) + gb_ref[...]
    c = hg[:, :C_WIDTH] * jax.nn.sigmoid(hg[:, C_WIDTH:])
    mix = (jnp.dot(a_ref[...].astype(BF16), wo_ref[0:A_WIDTH, :], preferred_element_type=F32)
           + jnp.dot(b_ref[...].astype(BF16), wo_ref[A_WIDTH:A_WIDTH + B_WIDTH, :], preferred_element_type=F32)
           + jnp.dot(c.astype(BF16), wo_ref[A_WIDTH + B_WIDTH:, :], preferred_element_type=F32))
    o_ref[...] = x_ref[...] + gt_ref[0] * mix


def _out_proj(X, mod3, a_o, b_o, pu, y2, s5_d, glu_w, glu_b, w_out, nb, lc):
    tt = X.shape[0]
    d = D_MODEL
    tm = ROW_TILE
    nct = lc // tm
    const = lambda shape: pl.BlockSpec(shape, lambda i, b: (0,) * len(shape), pipeline_mode=pl.Buffered(1))
    tile = lambda w: pl.BlockSpec((tm, w), lambda i, b: (i, b))
    y3 = y2.reshape(2, tt, nb * C_WIDTH)
    return pl.pallas_call(
        _out_kernel,
        grid=(tt // tm, nb),
        in_specs=[tile(d), pl.BlockSpec((1, 1, d), _who_map(nct, nb, 2)),
                  tile(A_WIDTH), tile(B_WIDTH), tile(C_WIDTH),
                  pl.BlockSpec((1, tm, C_WIDTH), lambda i, b: (0, i, b)),
                  pl.BlockSpec((1, tm, C_WIDTH), lambda i, b: (1, i, b)),
                  const((1, C_WIDTH)), const((C_WIDTH, 2 * C_WIDTH)), const((1, 2 * C_WIDTH)),
                  const((D_MIX, d))],
        out_specs=tile(d),
        out_shape=jax.ShapeDtypeStruct(X.shape, F32),
        input_output_aliases={0: 0},
        compiler_params=_cparams(2),
        name="out_proj",
    )(X, mod3, a_o, b_o, pu, y3, y3, s5_d.reshape(1, C_WIDTH), glu_w, glu_b.reshape(1, 2 * C_WIDTH), w_out)


def _final_kernel(x_ref, g_ref, o_ref):
    x = x_ref[...]
    o_ref[0] = x * lax.rsqrt(jnp.mean(x * x, axis=-1, keepdims=True) + RMS_EPS) * g_ref[...]


def _final_norm(X, g, nb, lc):
    tt = X.shape[0]
    d = D_MODEL
    tm = ROW_TILE
    nct = lc // tm
    return pl.pallas_call(
        _final_kernel,
        grid=(nb, (tt - lc) // tm),
        in_specs=[pl.BlockSpec((tm, d), lambda b, i: (i + nct, b)),
                  pl.BlockSpec((1, d), lambda b, i: (0, 0))],
        out_specs=pl.BlockSpec((1, tm, d), lambda b, i: (b, i, 0)),
        out_shape=jax.ShapeDtypeStruct((nb, tt - lc, d), F32),
        compiler_params=_cparams(2),
        name="final_norm",
    )(X, g.reshape(1, d))


def _rope_tables(length, lc):
    t = jnp.arange(length)
    pos = jnp.stack([(t // GRID_W).astype(F32), (t % GRID_W).astype(F32)], axis=1)
    inv_freq = ROPE_BASE ** (-jnp.arange(ROPE_PAIRS, dtype=F32) / ROPE_PAIRS)
    ang = pos[:, :, None] * inv_freq
    cos = jnp.cos(ang)[:, :, None, :]
    sin = jnp.sin(ang)[:, :, None, :]
    cos = jnp.broadcast_to(cos, (length, 2, 2, ROPE_PAIRS)).reshape(length, B_HEAD_DIM)
    sin = jnp.concatenate([-sin, sin], axis=2).reshape(length, B_HEAD_DIM)
    cos = jnp.concatenate([jnp.ones((lc, B_HEAD_DIM), F32), cos], axis=0)
    sin = jnp.concatenate([jnp.zeros((lc, B_HEAD_DIM), F32), sin], axis=0)
    return jnp.tile(cos, (1, 128 // B_HEAD_DIM)), jnp.tile(sin, (1, 128 // B_HEAD_DIM))


def kernel(x, c, ctx, c_ctx, ada_w, ada_b, norm_g, ffn_w1, ffn_w2, w_in, w_out, hgrn_lower_bounds,
           hgrn_norm_g, attn_sink, s5_a_re, s5_a_im, s5_log_dt, s5_b_re, s5_b_im, s5_c_re, s5_c_im,
           s5_d, s5_glu_w, s5_glu_b, final_norm_g):
    nb, length, d = x.shape
    lc = ctx.shape[1]
    tt = lc + length
    depth = ada_w.shape[0]
    assert d == D_MODEL and nb == 8 and lc % ROW_TILE == 0 and length % ROW_TILE == 0

    X = jnp.concatenate([ctx, x], axis=1).transpose(1, 0, 2).reshape(tt, nb * d)
    act = jnp.concatenate([c, c_ctx[None], jnp.zeros((16 - nb - 1, d), F32)], axis=0)
    mod = _modulation(act, ada_w, ada_b).reshape(depth, 16, 3, 3, d)[:, :nb + 1]
    cos_t, sin_t = _rope_tables(length, lc)
    lb_soft = jax.nn.softmax(hgrn_lower_bounds.astype(F32), axis=0)
    lower_bound = jnp.cumsum(lb_soft, axis=0) - lb_soft[0]

    for l in range(depth):
        mod3 = [mod[l, :, i].reshape((nb + 1) * 3, 1, d) for i in range(3)]
        w1 = [ffn_w1[l, i].astype(BF16) for i in range(2)]
        w2 = [ffn_w2[l, i].astype(BF16) for i in range(2)]
        X = _ffn(X, mod3[0], norm_g[l, 0], w1[0], w2[0], nb, lc)
        pa, pq, pk, pv, pu = _proj(X, mod3[1], norm_g[l, 1], w_in[l].astype(BF16), cos_t, sin_t, nb, lc)
        a_o = _hgrn(pa, lower_bound[l], hgrn_norm_g[l], nb, lc)
        b_o = _attn(pq, pk, pv, attn_sink[l], nb, lc)
        s5a, s5wb, s5wc = _s5_params(s5_a_re[l], s5_a_im[l], s5_log_dt[l], s5_b_re[l], s5_b_im[l],
                                     s5_c_re[l], s5_c_im[l])
        y2 = _s5_scan(pu.reshape(tt * nb, C_WIDTH), s5a, s5wb, s5wc, nb, lc)
        X = _out_proj(X, mod3[1], a_o, b_o, pu, y2, s5_d[l], s5_glu_w[l].astype(BF16), s5_glu_b[l],
                      w_out[l].astype(BF16), nb, lc)
        X = _ffn(X, mod3[2], norm_g[l, 2], w1[1], w2[1], nb, lc)
    return _final_norm(X, final_norm_g, nb, lc)
```

```python
import functools
import math

import jax
import jax.numpy as jnp
from jax import lax
from jax.experimental import pallas as pl
from jax.experimental.pallas import tpu as pltpu

F32 = jnp.float32
BF16 = jnp.bfloat16

D_MODEL = 1024
DEPTH = 4
GRID_W = 64
RMS_EPS = 1e-6
N_MOD = 9
D_FF = 2816
A_HEADS = 4
A_HEAD_DIM = 64
A_WIDTH = A_HEADS * A_HEAD_DIM
GLR_CHUNK = 16
B_Q_HEADS = 8
B_KV_HEADS = 2
B_GROUP = B_Q_HEADS // B_KV_HEADS
B_HEAD_DIM = 64
B_WIDTH = B_Q_HEADS * B_HEAD_DIM
B_KV_WIDTH = B_KV_HEADS * B_HEAD_DIM
WINDOW = 128
ATTN_BLOCK = 128
ROPE_BASE = 10000.0
ROPE_PAIRS = B_HEAD_DIM // 4
MASK_VALUE = -1e9
C_GROUPS = 16
C_GROUP_CH = 16
C_WIDTH = C_GROUPS * C_GROUP_CH
C_STATE = 64
C_NSTATE = C_GROUPS * C_STATE
D_MIX = A_WIDTH + B_WIDTH + C_WIDTH
A_IN = 5 * A_WIDTH
D_IN = A_IN + B_WIDTH + 2 * B_KV_WIDTH + C_WIDTH

V7X_VMEM_LIMIT = 56 * 1024 * 1024
ROW_TILE = 256
S5_TILE = 128
HGRN_CHUNK = 32
HGRN_SUB = 8


def _cparams(n_axes, vmem=None):
    return pltpu.CompilerParams(dimension_semantics=("arbitrary",) * n_axes, vmem_limit_bytes=vmem)


def _split_hi_lo(x):
    hi = x.astype(BF16)
    lo = (x - hi.astype(F32)).astype(BF16)
    return hi, lo


def _mod_kernel(act_ref, w_ref, b_ref, o_ref):
    a = act_ref[...]
    a = (a * jax.nn.sigmoid(a)).astype(BF16)
    o_ref[0] = jnp.dot(a, w_ref[0].astype(BF16), preferred_element_type=F32) + b_ref[0]


def _modulation(act, ada_w, ada_b):
    depth, d, n = ada_w.shape
    tn = 1024
    return pl.pallas_call(
        _mod_kernel,
        grid=(depth, n // tn),
        in_specs=[pl.BlockSpec((16, d), lambda l, j: (0, 0)),
                  pl.BlockSpec((1, d, tn), lambda l, j: (l, 0, j)),
                  pl.BlockSpec((1, 1, tn), lambda l, j: (l, 0, j))],
        out_specs=pl.BlockSpec((1, 16, tn), lambda l, j: (l, 0, j)),
        out_shape=jax.ShapeDtypeStruct((depth, 16, n), F32),
        compiler_params=_cparams(2),
        name="adaln_mod",
    )(act, ada_w, ada_b.reshape(depth, 1, n))


def _adaln(x, g, shift, scale):
    r = lax.rsqrt(jnp.mean(x * x, axis=-1, keepdims=True) + RMS_EPS)
    return (x * r) * (g * (1.0 + scale)) + shift


def _who_map(nct, nb, comp):
    def index_map(i, b):
        return (jnp.where(i < nct, nb, b) * 3 + comp, 0, 0)
    return index_map


def _ffn_kernel(x_ref, sh_ref, sc_ref, gt_ref, g_ref, w1_ref, w2_ref, o_ref):
    x = x_ref[...]
    h = _adaln(x, g_ref[...], sh_ref[0], sc_ref[0]).astype(BF16)
    gu = jnp.dot(h, w1_ref[...], preferred_element_type=F32)
    gate = gu[:, :D_FF]
    a = (gate * jax.nn.sigmoid(gate) * gu[:, D_FF:]).astype(BF16)
    y = jnp.dot(a, w2_ref[...], preferred_element_type=F32)
    o_ref[...] = x + (0.5 * gt_ref[0]) * y


def _ffn(X, mod3, g, w1, w2, nb, lc):
    tt = X.shape[0]
    d = D_MODEL
    tm = ROW_TILE
    nct = lc // tm
    mspec = lambda comp: pl.BlockSpec((1, 1, d), _who_map(nct, nb, comp))
    const = lambda shape: pl.BlockSpec(shape, lambda i, b: (0,) * len(shape), pipeline_mode=pl.Buffered(1))
    return pl.pallas_call(
        _ffn_kernel,
        grid=(tt // tm, nb),
        in_specs=[pl.BlockSpec((tm, d), lambda i, b: (i, b)), mspec(0), mspec(1), mspec(2),
                  const((1, d)), const((d, 2 * D_FF)), const((D_FF, d))],
        out_specs=pl.BlockSpec((tm, d), lambda i, b: (i, b)),
        out_shape=jax.ShapeDtypeStruct(X.shape, F32),
        input_output_aliases={0: 0},
        compiler_params=_cparams(2, V7X_VMEM_LIMIT),
        name="ffn_half_step",
    )(X, mod3, mod3, mod3, g.reshape(1, d), w1, w2)


def _rope(x, cos, sin, first_half):
    w = x.shape[-1]
    nxt = pltpu.roll(x, w - ROPE_PAIRS, axis=1)
    prv = pltpu.roll(x, ROPE_PAIRS, axis=1)
    return x * cos + jnp.where(first_half, nxt, prv) * sin


def _proj_kernel(x_ref, sh_ref, sc_ref, g_ref, w_ref, cos_ref, sin_ref,
                 pa_ref, pq_ref, pk_ref, pv_ref, pu_ref):
    x = x_ref[...]
    h = _adaln(x, g_ref[...], sh_ref[0], sc_ref[0]).astype(BF16)
    p = jnp.dot(h, w_ref[...], preferred_element_type=F32)
    o = A_IN
    pa_ref[...] = p[:, :o]
    cos = cos_ref[...]
    sin = sin_ref[...]
    lane = lax.broadcasted_iota(jnp.int32, (1, B_WIDTH), 1)
    first_half = (lane % (2 * ROPE_PAIRS)) < ROPE_PAIRS
    cos_q = jnp.concatenate([cos] * (B_WIDTH // 128), axis=1)
    sin_q = jnp.concatenate([sin] * (B_WIDTH // 128), axis=1)
    q = _rope(p[:, o:o + B_WIDTH], cos_q, sin_q, first_half)
    pq_ref[...] = (q * (B_HEAD_DIM ** -0.5)).astype(BF16)
    o += B_WIDTH
    k = _rope(p[:, o:o + B_KV_WIDTH], cos, sin, first_half[:, :B_KV_WIDTH])
    pk_ref[...] = k.astype(BF16)
    o += B_KV_WIDTH
    pv_ref[...] = p[:, o:o + B_KV_WIDTH].astype(BF16)
    o += B_KV_WIDTH
    pu_ref[...] = p[:, o:o + C_WIDTH]


def _proj(X, mod3, g, w_in, cos_t, sin_t, nb, lc):
    tt = X.shape[0]
    d = D_MODEL
    tm = ROW_TILE
    nct = lc // tm
    mspec = lambda comp: pl.BlockSpec((1, 1, d), _who_map(nct, nb, comp))
    const = lambda shape: pl.BlockSpec(shape, lambda i, b: (0,) * len(shape), pipeline_mode=pl.Buffered(1))
    tile = lambda w: pl.BlockSpec((tm, w), lambda i, b: (i, b))
    widths = (A_IN, B_WIDTH, B_KV_WIDTH, B_KV_WIDTH, C_WIDTH)
    dtypes = (F32, BF16, BF16, BF16, F32)
    return pl.pallas_call(
        _proj_kernel,
        grid=(tt // tm, nb),
        in_specs=[tile(d), mspec(0), mspec(1), const((1, d)), const((d, D_IN)),
                  pl.BlockSpec((tm, 128), lambda i, b: (i, 0)), pl.BlockSpec((tm, 128), lambda i, b: (i, 0))],
        out_specs=[tile(w) for w in widths],
        out_shape=[jax.ShapeDtypeStruct((tt, nb * w), dt) for w, dt in zip(widths, dtypes)],
        compiler_params=_cparams(2, V7X_VMEM_LIMIT),
        name="in_proj",
    )(X, mod3, mod3, g.reshape(1, d), w_in, cos_t, sin_t)


def _hgrn_pairs(lo, hi, reverse):
    if hi - lo <= HGRN_SUB:
        return []
    mid = (lo + hi) // 2
    here = (lo, mid, mid - lo, mid) if reverse else (mid, lo, mid - lo, mid - 1)
    return [here] + _hgrn_pairs(lo, mid, reverse) + _hgrn_pairs(mid, hi, reverse)


def _hgrn_kernel(pa_ref, lb_ref, ng_ref, o_ref,
                 qs_s, kf_s, kb_s, bf_s, bb_s, v_s, o_s, p_s, st_s, mask_s, ones_s, tri_s, *, lc):
    tt = pa_ref.shape[0]
    w = A_WIDTH
    ch = HGRN_CHUNK
    sub = HGRN_SUB
    n_sub = ch // sub
    blk = 128
    n_blk = tt // blk
    n_chunks = tt // ch
    n_ctx_chunks = lc // ch
    nt = (((1,), (1,)), ((), ()))
    tn = (((0,), (0,)), ((), ()))

    row = lax.broadcasted_iota(jnp.int32, (w, w), 0)
    col = lax.broadcasted_iota(jnp.int32, (w, w), 1)
    same_head = jnp.where((row // A_HEAD_DIM) == (col // A_HEAD_DIM), 1.0, 0.0)
    mask_s[...] = same_head
    ones_s[...] = same_head.astype(BF16)
    lane = lax.broadcasted_iota(jnp.int32, (1, w), 1)
    head_mask = [jnp.where(lane // A_HEAD_DIM == h, 1.0, 0.0) for h in range(A_HEADS)]
    r = lax.broadcasted_iota(jnp.int32, (blk, blk), 0)
    c = lax.broadcasted_iota(jnp.int32, (blk, blk), 1)
    same_chunk = (r // ch) == (c // ch)
    tri_s[0] = jnp.where(same_chunk, jnp.where(c <= r, 1.0, 0.0), 0.0).astype(BF16)
    tri_s[1] = jnp.where(same_chunk, jnp.where(c >= r, 1.0, 0.0), 0.0).astype(BF16)
    st_s[...] = jnp.zeros(st_s.shape, F32)

    def chunk_cumsum(logf, tri):
        hi = logf.astype(BF16)
        rem = logf - hi.astype(F32)
        mid = rem.astype(BF16)
        lo = (rem - mid.astype(F32)).astype(BF16)
        return (jnp.dot(tri, hi, preferred_element_type=F32)
                + jnp.dot(tri, mid, preferred_element_type=F32)
                + jnp.dot(tri, lo, preferred_element_type=F32))

    def load(ref, rows):
        return jnp.concatenate([ref[c, rows, :] for c in range(w // 128)], axis=1)

    def store(ref, rows, val):
        for c in range(w // 128):
            ref[c, rows, :] = val[:, c * 128:(c + 1) * 128]

    def load_row(ref, r):
        return load(ref, pl.ds(r, sub, stride=0))

    def prep_body(i, carry):
        rows = pl.ds(pl.multiple_of(i * blk, blk), blk)
        q = pa_ref[rows, 0:w]
        qs_s[rows, :] = q * jax.nn.sigmoid(q)
        store(v_s, rows, pa_ref[rows, w:2 * w])
        o_s[rows, :] = jnp.zeros((blk, w), F32)
        for d, (k_s, b_s) in enumerate(((kf_s, bf_s), (kb_s, bb_s))):
            lb = lb_ref[d:d + 1, :]
            z = pa_ref[rows, (2 + d) * w:(3 + d) * w]
            f = lb + (1.0 - lb) * jax.nn.sigmoid(z)
            store(k_s, rows, (1.0 - lb) * jax.nn.sigmoid(-z))
            store(b_s, rows, chunk_cumsum(jnp.log(f), tri_s[d]))
        return carry
    lax.fori_loop(0, n_blk, prep_body, 0)

    def expand(x):
        return jnp.concatenate([x * head_mask[h] for h in range(A_HEADS)], axis=0).astype(BF16)

    def diag_scores(r0):
        t_i = lax.broadcasted_iota(jnp.int32, (sub, w), 0)
        for j in range(n_sub):
            base = r0 + j * sub
            rows = pl.ds(base, sub)
            q = qs_s[rows, :]
            bf = load(bf_s, rows)
            bb = load(bb_s, rows)
            for s in range(sub):
                kf_row = load_row(kf_s, base + s)
                kb_row = load_row(kb_s, base + s)
                arg = jnp.where(t_i >= s, bf - load_row(bf_s, base + s), bb - load_row(bb_s, base + s))
                kk = jnp.where(t_i > s, kf_row, jnp.where(t_i < s, kb_row, kf_row + kb_row))
                p_s[(j * sub + s) * sub:(j * sub + s + 1) * sub, :] = q * jnp.exp(arg) * kk
        return jnp.dot(p_s[...].astype(BF16), ones_s[...], preferred_element_type=F32)

    def diag_apply(r0, a):
        out = []
        for j in range(n_sub):
            acc = None
            for s in range(sub):
                v_row = load_row(v_s, r0 + j * sub + s)
                term = a[(j * sub + s) * sub:(j * sub + s + 1) * sub, :] * v_row
                acc = term if acc is None else acc + term
            out.append(acc)
        return out

    def dir_scores(r0, d):
        k_s, b_s = (kf_s, bf_s) if d == 0 else (kb_s, bb_s)
        rows = pl.ds(r0, ch)
        b = load(b_s, rows)
        q = qs_s[rows, :]
        k = load(k_s, rows)
        v = load(v_s, rows)
        last = (ch - 1) if d == 0 else 0
        b_last = b[last:last + 1, :]
        pairs = _hgrn_pairs(0, ch, d == 1)
        scores = []
        for t0, s0, n, ref in pairs:
            b_ref = b[ref:ref + 1, :]
            qx = (q[t0:t0 + n, :] * jnp.exp(b[t0:t0 + n, :] - b_ref)).astype(BF16)
            kx = expand(k[s0:s0 + n, :] * jnp.exp(b_ref - b[s0:s0 + n, :]))
            scores.append(lax.dot_general(qx, kx, nt, preferred_element_type=F32))
        st = st_s[d]
        qe = (q * jnp.exp(b)).astype(BF16)
        kd = (k * jnp.exp(b_last - b)).astype(BF16)
        o = lax.dot_general(qe, st.astype(BF16), nt, preferred_element_type=F32)
        ut = lax.dot_general(v.astype(BF16), kd, tn, preferred_element_type=F32)
        return pairs, scores, v, o, ut, st, b_last

    def dir_values(pairs, scores, v):
        return [jnp.dot(sc.astype(BF16), expand(v[s0:s0 + n, :]), preferred_element_type=F32)
                for (t0, s0, n, ref), sc in zip(pairs, scores)]

    def dir_finish(d, pairs, contribs, o, ut, st, b_last):
        st_s[d] = st * jnp.exp(b_last) + ut * mask_s[...]
        out = [o[j * sub:(j + 1) * sub, :] for j in range(n_sub)]
        for (t0, s0, n, ref), contrib in zip(pairs, contribs):
            for jj in range(n // sub):
                out[t0 // sub + jj] = out[t0 // sub + jj] + contrib[jj * sub:(jj + 1) * sub, :]
        return out

    def chunk_body(i, carry):
        cb = jnp.where(i < n_ctx_chunks, n_ctx_chunks - 1 - i, n_chunks + n_ctx_chunks - 1 - i)
        rf = pl.multiple_of(i * ch, ch)
        rb = pl.multiple_of(cb * ch, ch)
        pf, scf, vf, of, utf, stf, blf = dir_scores(rf, 0)
        pb, scb, vb, ob, utb, stb, blb = dir_scores(rb, 1)
        a = diag_scores(rf)
        cf = dir_values(pf, scf, vf)
        cbw = dir_values(pb, scb, vb)
        o_fwd = dir_finish(0, pf, cf, of, utf, stf, blf)
        o_bwd = dir_finish(1, pb, cbw, ob, utb, stb, blb)
        o_diag = diag_apply(rf, a)
        rows_f = pl.ds(rf, ch)
        o_s[rows_f, :] = o_s[rows_f, :] + jnp.concatenate([x + y for x, y in zip(o_diag, o_fwd)], axis=0)
        rows_b = pl.ds(rb, ch)
        o_s[rows_b, :] = o_s[rows_b, :] + jnp.concatenate(o_bwd, axis=0)
        return carry
    lax.fori_loop(0, n_chunks, chunk_body, 0)

    def out_body(i, carry):
        rows = pl.ds(pl.multiple_of(i * blk, blk), blk)
        o = o_s[rows, :]
        hi, lo = _split_hi_lo(o * o)
        ones = ones_s[...]
        ms = (jnp.dot(hi, ones, preferred_element_type=F32)
              + jnp.dot(lo, ones, preferred_element_type=F32)) * (1.0 / A_HEAD_DIM)
        g = pa_ref[rows, 4 * w:5 * w]
        o_ref[rows, :] = o * lax.rsqrt(ms + RMS_EPS) * ng_ref[...] * (g * jax.nn.sigmoid(g))
        return carry
    lax.fori_loop(0, n_blk, out_body, 0)


def _hgrn(pa, lower_bound, norm_g, nb, lc):
    tt = pa.shape[0]
    w = A_WIDTH
    seq = lambda: pltpu.VMEM((tt, w), F32)
    split = lambda: pltpu.VMEM((w // 128, tt, 128), F32)
    return pl.pallas_call(
        functools.partial(_hgrn_kernel, lc=lc),
        grid=(nb,),
        in_specs=[pl.BlockSpec((tt, A_IN), lambda b: (0, b)),
                  pl.BlockSpec((2, w), lambda b: (0, 0)),
                  pl.BlockSpec((1, w), lambda b: (0, 0))],
        out_specs=pl.BlockSpec((tt, w), lambda b: (0, b)),
        out_shape=jax.ShapeDtypeStruct((tt, nb * w), F32),
        scratch_shapes=[seq(), split(), split(), split(), split(), split(), seq(),
                        pltpu.VMEM((HGRN_CHUNK * HGRN_SUB, w), F32),
                        pltpu.VMEM((2, w, w), F32), pltpu.VMEM((w, w), F32), pltpu.VMEM((w, w), BF16),
                        pltpu.VMEM((2, 128, 128), BF16)],
        compiler_params=_cparams(1, V7X_VMEM_LIMIT),
        name="hgrn2_mixer",
    )(pa, lower_bound, norm_g.reshape(1, w))


def _attn_kernel(sink_ref, q_ref, k_ref, v_ref, o_ref, *, lc):
    tt = k_ref.shape[0]
    blk = ATTN_BLOCK
    hd = B_HEAD_DIM
    j = pl.program_id(1)
    n_ctx_blk = lc // blk
    n_blk = tt // blk
    nt = (((1,), (1,)), ((), ()))

    def head_slices(h):
        g = h // B_GROUP
        return slice(h * hd, (h + 1) * hd), slice(g * hd, (g + 1) * hd)

    def finish(h, pieces, vs):
        sink = sink_ref[h]
        m = sink
        for s in pieces:
            m = jnp.maximum(m, jnp.max(s, axis=-1, keepdims=True))
        den = jnp.exp(sink - m)
        acc = None
        for s, vv in zip(pieces, vs):
            p = jnp.exp(s - m)
            den = den + jnp.sum(p, axis=-1, keepdims=True)
            pv = jnp.dot(p.astype(BF16), vv, preferred_element_type=F32)
            acc = pv if acc is None else acc + pv
        return acc / den

    @pl.when(j < n_ctx_blk)
    def _():
        for h in range(B_Q_HEADS):
            qs, ks = head_slices(h)
            q = q_ref[:, qs]
            s = lax.dot_general(q, k_ref[0:lc, ks], nt, preferred_element_type=F32)
            o_ref[:, qs] = finish(h, [s], [v_ref[0:lc, ks]])

    @pl.when(j >= n_ctx_blk)
    def _():
        r_prev = pl.multiple_of((j - 1) * blk, blk)
        r_own = pl.multiple_of(j * blk, blk)
        r_next = pl.multiple_of(jnp.minimum(j + 1, n_blk - 1) * blk, blk)
        row = lax.broadcasted_iota(jnp.int32, (blk, blk), 0)
        col = lax.broadcasted_iota(jnp.int32, (blk, blk), 1)
        keep_prev = col >= row + jnp.where(j > n_ctx_blk, 0, blk)
        keep_next = col <= row - jnp.where(j < n_blk - 1, 0, blk)
        for h in range(B_Q_HEADS):
            qs, ks = head_slices(h)
            q = q_ref[:, qs]
            sp = lax.dot_general(q, k_ref[pl.ds(r_prev, blk), ks], nt, preferred_element_type=F32)
            so = lax.dot_general(q, k_ref[pl.ds(r_own, blk), ks], nt, preferred_element_type=F32)
            sn = lax.dot_general(q, k_ref[pl.ds(r_next, blk), ks], nt, preferred_element_type=F32)
            sc = lax.dot_general(q, k_ref[0:lc, ks], nt, preferred_element_type=F32)
            sp = jnp.where(keep_prev, sp, MASK_VALUE)
            sn = jnp.where(keep_next, sn, MASK_VALUE)
            vs = [v_ref[pl.ds(r_prev, blk), ks], v_ref[pl.ds(r_own, blk), ks],
                  v_ref[pl.ds(r_next, blk), ks], v_ref[0:lc, ks]]
            o_ref[:, qs] = finish(h, [sp, so, sn, sc], vs)


def _attn(pq, pk, pv, sink, nb, lc):
    tt = pq.shape[0]
    blk = ATTN_BLOCK
    return pl.pallas_call(
        functools.partial(_attn_kernel, lc=lc),
        grid=(nb, tt // blk),
        in_specs=[pl.BlockSpec(memory_space=pltpu.SMEM),
                  pl.BlockSpec((blk, B_WIDTH), lambda b, j: (j, b)),
                  pl.BlockSpec((tt, B_KV_WIDTH), lambda b, j: (0, b)),
                  pl.BlockSpec((tt, B_KV_WIDTH), lambda b, j: (0, b))],
        out_specs=pl.BlockSpec((blk, B_WIDTH), lambda b, j: (j, b)),
        out_shape=jax.ShapeDtypeStruct((tt, nb * B_WIDTH), F32),
        compiler_params=_cparams(2),
        name="window_gqa",
    )(sink, pq, pk, pv)


def _s5_tile_index(d, i, n_ctx, n_all):
    rev = jnp.where(i < n_ctx, n_ctx - 1 - i, n_all + n_ctx - 1 - i)
    return jnp.where(d == 0, i, rev)


def _s5_kernel(u_ref, a_ref, wb_ref, wc_ref, y_ref, bu_s, h_s, *, nb):
    d = pl.program_id(0)
    i = pl.program_id(1)
    n = C_NSTATE
    steps = u_ref.shape[0] // nb

    @pl.when(i == 0)
    def _():
        h_s[...] = jnp.zeros(h_s.shape, F32)

    bu_s[...] = jnp.dot(u_ref[...].astype(BF16), wb_ref[0], preferred_element_type=F32)
    a_re = jnp.broadcast_to(a_ref[0, 0:1, :], (nb, n))
    a_im = jnp.broadcast_to(a_ref[0, 1:2, :], (nb, n))

    def step(t, carry):
        h_re, h_im = carry
        tt = jnp.where(d == 0, t, steps - 1 - t)
        rows = pl.ds(pl.multiple_of(tt * nb, nb), nb)
        n_re = a_re * h_re - a_im * h_im + bu_s[rows, 0:n]
        n_im = a_re * h_im + a_im * h_re + bu_s[rows, n:2 * n]
        bu_s[rows, 0:n] = n_re
        bu_s[rows, n:2 * n] = n_im
        return n_re, n_im

    h_re, h_im = lax.fori_loop(0, steps, step, (h_s[:, 0:n], h_s[:, n:2 * n]), unroll=4)
    h_s[:, 0:n] = h_re
    h_s[:, n:2 * n] = h_im
    y_ref[0] = jnp.dot(bu_s[...].astype(BF16), wc_ref[0], preferred_element_type=F32)


def _s5_scan(u_flat, a, wb, wc, nb, lc):
    rows = u_flat.shape[0]
    tile_rows = S5_TILE * nb
    n_all = rows // tile_rows
    n_ctx = (lc * nb) // tile_rows
    tile = lambda d, i: _s5_tile_index(d, i, n_ctx, n_all)
    return pl.pallas_call(
        functools.partial(_s5_kernel, nb=nb),
        grid=(2, n_all),
        in_specs=[pl.BlockSpec((tile_rows, C_WIDTH), lambda d, i: (tile(d, i), 0)),
                  pl.BlockSpec((1, 2, C_NSTATE), lambda d, i: (d, 0, 0)),
                  pl.BlockSpec((1, C_WIDTH, 2 * C_NSTATE), lambda d, i: (d, 0, 0)),
                  pl.BlockSpec((1, 2 * C_NSTATE, C_WIDTH), lambda d, i: (d, 0, 0))],
        out_specs=pl.BlockSpec((1, tile_rows, C_WIDTH), lambda d, i: (d, tile(d, i), 0)),
        out_shape=jax.ShapeDtypeStruct((2, rows, C_WIDTH), F32),
        scratch_shapes=[pltpu.VMEM((tile_rows, 2 * C_NSTATE), F32), pltpu.VMEM((nb, 2 * C_NSTATE), F32)],
        compiler_params=_cparams(2, V7X_VMEM_LIMIT),
        name="s5_scan",
    )(u_flat, a, wb, wc)


def _s5_params(a_re, a_im, log_dt, b_re, b_im, c_re, c_im):
    eye = jnp.eye(C_GROUPS, dtype=F32)
    a_out, wb_out, wc_out = [], [], []
    for k in range(2):
        dt = jnp.exp(log_dt[k])[:, None]
        mag = jnp.exp(a_re[k] * dt)
        ang = a_im[k] * dt
        abar_re, abar_im = mag * jnp.cos(ang), mag * jnp.sin(ang)
        den = a_re[k] * a_re[k] + a_im[k] * a_im[k]
        coef_re = ((abar_re - 1.0) * a_re[k] + abar_im * a_im[k]) / den
        coef_im = (abar_im * a_re[k] - (abar_re - 1.0) * a_im[k]) / den
        bbar_re = coef_re[..., None] * b_re - coef_im[..., None] * b_im
        bbar_im = coef_re[..., None] * b_im + coef_im[..., None] * b_re
        a_out.append(jnp.stack([abar_re.reshape(-1), abar_im.reshape(-1)]))
        drive = lambda bb: jnp.einsum('gpc,gh->gchp', bb, eye).reshape(C_WIDTH, C_NSTATE)
        read = lambda cc: jnp.einsum('gcp,gh->gphc', cc, eye).reshape(C_NSTATE, C_WIDTH)
        wb_out.append(jnp.concatenate([drive(bbar_re), drive(bbar_im)], axis=1))
        wc_out.append(jnp.concatenate([read(c_re[k]), -read(c_im[k])], axis=0))
    return jnp.stack(a_out), jnp.stack(wb_out).astype(BF16), jnp.stack(wc_out).astype(BF16)


def _out_kernel(x_ref, gt_ref, a_ref, b_ref, u_ref, yf_ref, yb_ref, d_ref, gw_ref, gb_ref, wo_ref, o_ref):
    y = d_ref[...] * u_ref[...] + yf_ref[0] + yb_ref[0]
    gelu = 0.5 * y * (1.0 + jnp.tanh(math.sqrt(2.0 / math.pi) * (y + 0.044715 * (y * y * y))))
    hg = jnp.dot(gelu.astype(BF16), gw_ref[...], preferred_element_type=F32) + gb_ref[...]
    c = hg[:, :C_WIDTH] * jax.nn.sigmoid(hg[:, C_WIDTH:])
    mix = (jnp.dot(a_ref[...].astype(BF16), wo_ref[0:A_WIDTH, :], preferred_element_type=F32)
           + jnp.dot(b_ref[...].astype(BF16), wo_ref[A_WIDTH:A_WIDTH + B_WIDTH, :], preferred_element_type=F32)
           + jnp.dot(c.astype(BF16), wo_ref[A_WIDTH + B_WIDTH:, :], preferred_element_type=F32))
    o_ref[...] = x_ref[...] + gt_ref[0] * mix


def _out_proj(X, mod3, a_o, b_o, pu, y2, s5_d, glu_w, glu_b, w_out, nb, lc):
    tt = X.shape[0]
    d = D_MODEL
    tm = ROW_TILE
    nct = lc // tm
    const = lambda shape: pl.BlockSpec(shape, lambda i, b: (0,) * len(shape), pipeline_mode=pl.Buffered(1))
    tile = lambda w: pl.BlockSpec((tm, w), lambda i, b: (i, b))
    y3 = y2.reshape(2, tt, nb * C_WIDTH)
    return pl.pallas_call(
        _out_kernel,
        grid=(tt // tm, nb),
        in_specs=[tile(d), pl.BlockSpec((1, 1, d), _who_map(nct, nb, 2)),
                  tile(A_WIDTH), tile(B_WIDTH), tile(C_WIDTH),
                  pl.BlockSpec((1, tm, C_WIDTH), lambda i, b: (0, i, b)),
                  pl.BlockSpec((1, tm, C_WIDTH), lambda i, b: (1, i, b)),
                  const((1, C_WIDTH)), const((C_WIDTH, 2 * C_WIDTH)), const((1, 2 * C_WIDTH)),
                  const((D_MIX, d))],
        out_specs=tile(d),
        out_shape=jax.ShapeDtypeStruct(X.shape, F32),
        input_output_aliases={0: 0},
        compiler_params=_cparams(2),
        name="out_proj",
    )(X, mod3, a_o, b_o, pu, y3, y3, s5_d.reshape(1, C_WIDTH), glu_w, glu_b.reshape(1, 2 * C_WIDTH), w_out)


def _final_kernel(x_ref, g_ref, o_ref):
    x = x_ref[...]
    o_ref[0] = x * lax.rsqrt(jnp.mean(x * x, axis=-1, keepdims=True) + RMS_EPS) * g_ref[...]


def _final_norm(X, g, nb, lc):
    tt = X.shape[0]
    d = D_MODEL
    tm = ROW_TILE
    nct = lc // tm
    return pl.pallas_call(
        _final_kernel,
        grid=(nb, (tt - lc) // tm),
        in_specs=[pl.BlockSpec((tm, d), lambda b, i: (i + nct, b)),
                  pl.BlockSpec((1, d), lambda b, i: (0, 0))],
        out_specs=pl.BlockSpec((1, tm, d), lambda b, i: (b, i, 0)),
        out_shape=jax.ShapeDtypeStruct((nb, tt - lc, d), F32),
        compiler_params=_cparams(2),
        name="final_norm",
    )(X, g.reshape(1, d))


def _rope_tables(length, lc):
    t = jnp.arange(length)
    pos = jnp.stack([(t // GRID_W).astype(F32), (t % GRID_W).astype(F32)], axis=1)
    inv_freq = ROPE_BASE ** (-jnp.arange(ROPE_PAIRS, dtype=F32) / ROPE_PAIRS)
    ang = pos[:, :, None] * inv_freq
    cos = jnp.cos(ang)[:, :, None, :]
    sin = jnp.sin(ang)[:, :, None, :]
    cos = jnp.broadcast_to(cos, (length, 2, 2, ROPE_PAIRS)).reshape(length, B_HEAD_DIM)
    sin = jnp.concatenate([-sin, sin], axis=2).reshape(length, B_HEAD_DIM)
    cos = jnp.concatenate([jnp.ones((lc, B_HEAD_DIM), F32), cos], axis=0)
    sin = jnp.concatenate([jnp.zeros((lc, B_HEAD_DIM), F32), sin], axis=0)
    return jnp.tile(cos, (1, 128 // B_HEAD_DIM)), jnp.tile(sin, (1, 128 // B_HEAD_DIM))


def kernel(x, c, ctx, c_ctx, ada_w, ada_b, norm_g, ffn_w1, ffn_w2, w_in, w_out, hgrn_lower_bounds,
           hgrn_norm_g, attn_sink, s5_a_re, s5_a_im, s5_log_dt, s5_b_re, s5_b_im, s5_c_re, s5_c_im,
           s5_d, s5_glu_w, s5_glu_b, final_norm_g):
    nb, length, d = x.shape
    lc = ctx.shape[1]
    tt = lc + length
    depth = ada_w.shape[0]
    assert d == D_MODEL and nb == 8 and lc % ROW_TILE == 0 and length % ROW_TILE == 0

    X = jnp.concatenate([ctx, x], axis=1).transpose(1, 0, 2).reshape(tt, nb * d)
    act = jnp.concatenate([c, c_ctx[None], jnp.zeros((16 - nb - 1, d), F32)], axis=0)
    mod = _modulation(act, ada_w, ada_b).reshape(depth, 16, 3, 3, d)[:, :nb + 1]
    cos_t, sin_t = _rope_tables(length, lc)
    lb_soft = jax.nn.softmax(hgrn_lower_bounds.astype(F32), axis=0)
    lower_bound = jnp.cumsum(lb_soft, axis=0) - lb_soft[0]

    for l in range(depth):
        mod3 = [mod[l, :, i].reshape((nb + 1) * 3, 1, d) for i in range(3)]
        w1 = [ffn_w1[l, i].astype(BF16) for i in range(2)]
        w2 = [ffn_w2[l, i].astype(BF16) for i in range(2)]
        X = _ffn(X, mod3[0], norm_g[l, 0], w1[0], w2[0], nb, lc)
        pa, pq, pk, pv, pu = _proj(X, mod3[1], norm_g[l, 1], w_in[l].astype(BF16), cos_t, sin_t, nb, lc)
        a_o = _hgrn(pa, lower_bound[l], hgrn_norm_g[l], nb, lc)
        b_o = _attn(pq, pk, pv, attn_sink[l], nb, lc)
        s5a, s5wb, s5wc = _s5_params(s5_a_re[l], s5_a_im[l], s5_log_dt[l], s5_b_re[l], s5_b_im[l],
                                     s5_c_re[l], s5_c_im[l])
        y2 = _s5_scan(pu.reshape(tt * nb, C_WIDTH), s5a, s5wb, s5wc, nb, lc)
        X = _out_proj(X, mod3[1], a_o, b_o, pu, y2, s5_d[l], s5_glu_w[l].astype(BF16), s5_glu_b[l],
                      w_out[l].astype(BF16), nb, lc)
        X = _ffn(X, mod3[2], norm_g[l, 2], w1[1], w2[1], nb, lc)
    return _final_norm(X, final_norm_g, nb, lc)
```

```python
import functools
import math

import jax
import jax.numpy as jnp
from jax import lax
from jax.experimental import pallas as pl
from jax.experimental.pallas import tpu as pltpu

F32 = jnp.float32
BF16 = jnp.bfloat16

D_MODEL = 1024
DEPTH = 4
GRID_W = 64
RMS_EPS = 1e-6
N_MOD = 9
D_FF = 2816
A_HEADS = 4
A_HEAD_DIM = 64
A_WIDTH = A_HEADS * A_HEAD_DIM
GLR_CHUNK = 16
B_Q_HEADS = 8
B_KV_HEADS = 2
B_GROUP = B_Q_HEADS // B_KV_HEADS
B_HEAD_DIM = 64
B_WIDTH = B_Q_HEADS * B_HEAD_DIM
B_KV_WIDTH = B_KV_HEADS * B_HEAD_DIM
WINDOW = 128
ATTN_BLOCK = 128
ROPE_BASE = 10000.0
ROPE_PAIRS = B_HEAD_DIM // 4
MASK_VALUE = -1e9
LOG2E = math.log2(math.e)
C_GROUPS = 16
C_GROUP_CH = 16
C_WIDTH = C_GROUPS * C_GROUP_CH
C_STATE = 64
C_NSTATE = C_GROUPS * C_STATE
D_MIX = A_WIDTH + B_WIDTH + C_WIDTH
A_IN = 5 * A_WIDTH
D_IN = A_IN + B_WIDTH + 2 * B_KV_WIDTH + C_WIDTH

V7X_VMEM_LIMIT = 56 * 1024 * 1024
ROW_TILE = 256
TOKEN_TILE = 576
S5_TILE = 128
HGRN_CHUNK = 128
HGRN_SUB = 8


def _cparams(n_axes, vmem=None):
    return pltpu.CompilerParams(dimension_semantics=("arbitrary",) * n_axes, vmem_limit_bytes=vmem)


def _split_hi_lo(x):
    hi = x.astype(BF16)
    lo = (x - hi.astype(F32)).astype(BF16)
    return hi, lo


def _mod_kernel(act_ref, w_ref, b_ref, o_ref):
    a = act_ref[...]
    a = (a * jax.nn.sigmoid(a)).astype(BF16)
    o_ref[0] = jnp.dot(a, w_ref[0].astype(BF16), preferred_element_type=F32) + b_ref[0]


def _modulation(act, ada_w, ada_b):
    depth, d, n = ada_w.shape
    tn = 1024
    return pl.pallas_call(
        _mod_kernel,
        grid=(depth, n // tn),
        in_specs=[pl.BlockSpec((16, d), lambda l, j: (0, 0)),
                  pl.BlockSpec((1, d, tn), lambda l, j: (l, 0, j)),
                  pl.BlockSpec((1, 1, tn), lambda l, j: (l, 0, j))],
        out_specs=pl.BlockSpec((1, 16, tn), lambda l, j: (l, 0, j)),
        out_shape=jax.ShapeDtypeStruct((depth, 16, n), F32),
        compiler_params=_cparams(2),
        name="adaln_mod",
    )(act, ada_w, ada_b.reshape(depth, 1, n))


def _is_ctx_row(tm, lc):
    return lax.broadcasted_iota(jnp.int32, (tm, D_MODEL), 0) < lc - pl.program_id(0) * tm


def _adaln(x, g, is_ctx, sh_c, sh_b, sc_c, sc_b):
    r = lax.rsqrt(jnp.mean(x * x, axis=-1, keepdims=True) + RMS_EPS)
    gain = jnp.where(is_ctx, g * (1.0 + sc_c[0]), g * (1.0 + sc_b[0]))
    return (x * r) * gain + jnp.where(is_ctx, sh_c[0], sh_b[0])


def _mod_specs(nb, comps):
    specs = []
    for comp in comps:
        specs.append(pl.BlockSpec((1, 1, D_MODEL), lambda i, b, comp=comp: (nb * 3 + comp, 0, 0)))
        specs.append(pl.BlockSpec((1, 1, D_MODEL), lambda i, b, comp=comp: (b * 3 + comp, 0, 0)))
    return specs


def _ffn_tile(x, lc, sh_c, sh_b, sc_c, sc_b, gt_c, gt_b, g_ref, w1_ref, w2_ref, o_ref):
    is_ctx = _is_ctx_row(x.shape[0], lc)
    h = _adaln(x, g_ref[...], is_ctx, sh_c, sh_b, sc_c, sc_b).astype(BF16)
    gu = jnp.dot(h, w1_ref[...], preferred_element_type=F32)
    gate = gu[:, :D_FF]
    a = (gate * jax.nn.sigmoid(gate) * gu[:, D_FF:]).astype(BF16)
    y = jnp.dot(a, w2_ref[...], preferred_element_type=F32)
    o_ref[...] = x + jnp.where(is_ctx, 0.5 * gt_c[0], 0.5 * gt_b[0]) * y


def _ffn_kernel(x_ref, *refs, lc):
    _ffn_tile(x_ref[...], lc, *refs)


def _ffn_entry_kernel(ctx_ref, lat_ref, *refs, lc):
    x = jnp.where(pl.program_id(0) * ctx_ref.shape[1] < lc, ctx_ref[0], lat_ref[0])
    _ffn_tile(x, lc, *refs)


def _layer_weight(stack, index):
    lead = len(index)
    return pl.BlockSpec((None,) * lead + stack.shape[lead:], lambda i, b: tuple(index) + (0, 0),
                        pipeline_mode=pl.Buffered(1))


def _ffn(X, mod3, g, w1s, w2s, index, nb, lc):
    d = D_MODEL
    rest = _mod_specs(nb, (0, 1, 2)) + [pl.BlockSpec((1, d), lambda i, b: (0, 0)),
                                        _layer_weight(w1s, index), _layer_weight(w2s, index)]
    if isinstance(X, tuple):
        ctx, lat = X
        tm = ROW_TILE
        nct = lc // tm
        tt = lc + lat.shape[1]
        body = functools.partial(_ffn_entry_kernel, lc=lc)
        srcs = [pl.BlockSpec((1, tm, d), lambda i, b: (b, jnp.minimum(i, nct - 1), 0)),
                pl.BlockSpec((1, tm, d), lambda i, b: (b, jnp.maximum(i - nct, 0), 0))]
        args, alias = (ctx, lat), {}
    else:
        tm = TOKEN_TILE
        tt = X.shape[0]
        body = functools.partial(_ffn_kernel, lc=lc)
        srcs, args, alias = [pl.BlockSpec((tm, d), lambda i, b: (i, b))], (X,), {0: 0}
    return pl.pallas_call(
        body,
        grid=(tt // tm, nb),
        in_specs=srcs + rest,
        out_specs=pl.BlockSpec((tm, d), lambda i, b: (i, b)),
        out_shape=jax.ShapeDtypeStruct((tt, nb * d), F32),
        input_output_aliases=alias,
        compiler_params=_cparams(2, V7X_VMEM_LIMIT),
        name="ffn_half_step",
    )(*args, *([mod3] * 6), g.reshape(1, d), w1s, w2s)


def _rope(x, cos, sin, first_half):
    w = x.shape[-1]
    nxt = pltpu.roll(x, w - ROPE_PAIRS, axis=1)
    prv = pltpu.roll(x, ROPE_PAIRS, axis=1)
    return x * cos + jnp.where(first_half, nxt, prv) * sin


def _proj_kernel(x_ref, sh_c, sh_b, sc_c, sc_b, g_ref, w_ref, cos_ref, sin_ref,
                 pa_ref, pq_ref, pqs_ref, pk_ref, pv_ref, pu_ref, *, lc):
    x = x_ref[...]
    h = _adaln(x, g_ref[...], _is_ctx_row(x.shape[0], lc), sh_c, sh_b, sc_c, sc_b).astype(BF16)
    p = jnp.dot(h, w_ref[...], preferred_element_type=F32)
    o = A_IN
    pa_ref[...] = p[:, :o]
    cos = cos_ref[...]
    sin = sin_ref[...]
    lane = lax.broadcasted_iota(jnp.int32, (1, B_WIDTH), 1)
    first_half = (lane % (2 * ROPE_PAIRS)) < ROPE_PAIRS
    cos_q = jnp.concatenate([cos] * (B_WIDTH // 128), axis=1)
    sin_q = jnp.concatenate([sin] * (B_WIDTH // 128), axis=1)
    q = _rope(p[:, o:o + B_WIDTH], cos_q, sin_q, first_half)
    q = q * (B_HEAD_DIM ** -0.5 * LOG2E)
    pq_ref[...] = q.astype(BF16)
    low_head = (lane % 128) < B_HEAD_DIM
    q_sw = jnp.where(low_head, pltpu.roll(q, B_WIDTH - B_HEAD_DIM, axis=1), pltpu.roll(q, B_HEAD_DIM, axis=1))
    pqs_ref[...] = q_sw.astype(BF16)
    o += B_WIDTH
    k = _rope(p[:, o:o + B_KV_WIDTH], cos, sin, first_half[:, :B_KV_WIDTH])
    pk_ref[...] = k.astype(BF16)
    o += B_KV_WIDTH
    pv_ref[...] = p[:, o:o + B_KV_WIDTH].astype(BF16)
    o += B_KV_WIDTH
    pu_ref[...] = p[:, o:o + C_WIDTH]


def _proj(X, mod3, g, w_in_s, layer, cos_t, sin_t, nb, lc):
    tt = X.shape[0]
    d = D_MODEL
    tm = TOKEN_TILE
    const = lambda shape: pl.BlockSpec(shape, lambda i, b: (0,) * len(shape), pipeline_mode=pl.Buffered(1))
    tile = lambda w: pl.BlockSpec((tm, w), lambda i, b: (i, b))
    widths = (A_IN, B_WIDTH, B_WIDTH, B_KV_WIDTH, B_KV_WIDTH, C_WIDTH)
    dtypes = (F32, BF16, BF16, BF16, BF16, F32)
    return pl.pallas_call(
        functools.partial(_proj_kernel, lc=lc),
        grid=(tt // tm, nb),
        in_specs=[tile(d)] + _mod_specs(nb, (0, 1)) + [
                  const((1, d)), _layer_weight(w_in_s, (layer,)),
                  pl.BlockSpec((tm, 128), lambda i, b: (i, 0)), pl.BlockSpec((tm, 128), lambda i, b: (i, 0))],
        out_specs=[tile(w) for w in widths],
        out_shape=[jax.ShapeDtypeStruct((tt, nb * w), dt) for w, dt in zip(widths, dtypes)],
        compiler_params=_cparams(2, V7X_VMEM_LIMIT),
        name="in_proj",
    )(X, mod3, mod3, mod3, mod3, g.reshape(1, d), w_in_s, cos_t, sin_t)


def _hgrn_pairs(lo, hi, reverse):
    if hi - lo <= HGRN_SUB:
        return []
    mid = (lo + hi) // 2
    here = (lo, mid, mid - lo, mid) if reverse else (mid, lo, mid - lo, mid - 1)
    return [here] + _hgrn_pairs(lo, mid, reverse) + _hgrn_pairs(mid, hi, reverse)


def _hgrn_kernel(pa_ref, lb_ref, ng_ref, o_ref,
                 qs_s, kf_s, kb_s, bf_s, bb_s, v_s, o_s, p_s, st_s, mask_s, ones_s, tri_s, *, lc):
    tt = pa_ref.shape[0]
    w = A_WIDTH
    ch = HGRN_CHUNK
    sub = HGRN_SUB
    n_sub = ch // sub
    blk = 128
    n_blk = tt // blk
    n_chunks = tt // ch
    n_ctx_chunks = lc // ch
    nt = (((1,), (1,)), ((), ()))
    tn = (((0,), (0,)), ((), ()))

    row = lax.broadcasted_iota(jnp.int32, (w, w), 0)
    col = lax.broadcasted_iota(jnp.int32, (w, w), 1)
    same_head = jnp.where((row // A_HEAD_DIM) == (col // A_HEAD_DIM), 1.0, 0.0)
    mask_s[...] = same_head
    ones_s[...] = same_head.astype(BF16)
    lane = lax.broadcasted_iota(jnp.int32, (1, w), 1)
    head_mask = [jnp.where(lane // A_HEAD_DIM == h, 1.0, 0.0) for h in range(A_HEADS)]
    r = lax.broadcasted_iota(jnp.int32, (blk, blk), 0)
    c = lax.broadcasted_iota(jnp.int32, (blk, blk), 1)
    same_chunk = (r // ch) == (c // ch)
    tri_s[0] = jnp.where(same_chunk, jnp.where(c <= r, 1.0, 0.0), 0.0).astype(BF16)
    tri_s[1] = jnp.where(same_chunk, jnp.where(c >= r, 1.0, 0.0), 0.0).astype(BF16)
    st_s[...] = jnp.zeros(st_s.shape, F32)

    def chunk_cumsum(logf, tri):
        hi = logf.astype(BF16)
        rem = logf - hi.astype(F32)
        mid = rem.astype(BF16)
        lo = (rem - mid.astype(F32)).astype(BF16)
        return (jnp.dot(tri, hi, preferred_element_type=F32)
                + jnp.dot(tri, mid, preferred_element_type=F32)
                + jnp.dot(tri, lo, preferred_element_type=F32))

    def load(ref, rows):
        return jnp.concatenate([ref[c, rows, :] for c in range(w // 128)], axis=1)

    def store(ref, rows, val):
        for c in range(w // 128):
            ref[c, rows, :] = val[:, c * 128:(c + 1) * 128]

    def load_row(ref, r):
        return load(ref, pl.ds(r, sub, stride=0))

    def prep_body(i, carry):
        rows = pl.ds(pl.multiple_of(i * blk, blk), blk)
        q = pa_ref[rows, 0:w]
        qs_s[rows, :] = q * jax.nn.sigmoid(q)
        store(v_s, rows, pa_ref[rows, w:2 * w])
        o_s[rows, :] = jnp.zeros((blk, w), F32)
        for d, (k_s, b_s) in enumerate(((kf_s, bf_s), (kb_s, bb_s))):
            lb = lb_ref[d:d + 1, :]
            z = pa_ref[rows, (2 + d) * w:(3 + d) * w]
            t = jnp.exp(-jnp.abs(z))
            big = 1.0 / (1.0 + t)
            small = t * big
            f = lb + (1.0 - lb) * jnp.where(z >= 0, big, small)
            store(k_s, rows, (1.0 - lb) * jnp.where(z >= 0, small, big))
            store(b_s, rows, chunk_cumsum(jnp.log(f), tri_s[d]))
        return carry
    lax.fori_loop(0, n_blk, prep_body, 0)

    def expand(x):
        return jnp.concatenate([x * head_mask[h] for h in range(A_HEADS)], axis=0).astype(BF16)

    def diag_scores(r0):
        t_i = lax.broadcasted_iota(jnp.int32, (sub, w), 0)
        for j in range(n_sub):
            base = r0 + j * sub
            rows = pl.ds(base, sub)
            q = qs_s[rows, :]
            bf = load(bf_s, rows)
            bb = load(bb_s, rows)
            for s in range(sub):
                kf_row = load_row(kf_s, base + s)
                kb_row = load_row(kb_s, base + s)
                arg = jnp.where(t_i >= s, bf - load_row(bf_s, base + s), bb - load_row(bb_s, base + s))
                kk = jnp.where(t_i > s, kf_row, jnp.where(t_i < s, kb_row, kf_row + kb_row))
                p_s[(j * sub + s) * sub:(j * sub + s + 1) * sub, :] = q * jnp.exp(arg) * kk
        return jnp.dot(p_s[...].astype(BF16), ones_s[...], preferred_element_type=F32)

    def diag_apply(r0, a):
        out = []
        for j in range(n_sub):
            acc = None
            for s in range(sub):
                v_row = load_row(v_s, r0 + j * sub + s)
                term = a[(j * sub + s) * sub:(j * sub + s + 1) * sub, :] * v_row
                acc = term if acc is None else acc + term
            out.append(acc)
        return out

    def dir_scores(r0, d):
        k_s, b_s = (kf_s, bf_s) if d == 0 else (kb_s, bb_s)
        rows = pl.ds(r0, ch)
        b = load(b_s, rows)
        q = qs_s[rows, :]
        k = load(k_s, rows)
        v = load(v_s, rows)
        last = (ch - 1) if d == 0 else 0
        b_last = b[last:last + 1, :]
        pairs = _hgrn_pairs(0, ch, d == 1)
        scores = []
        for t0, s0, n, ref in pairs:
            b_ref = b[ref:ref + 1, :]
            qx = (q[t0:t0 + n, :] * jnp.exp(b[t0:t0 + n, :] - b_ref)).astype(BF16)
            kx = expand(k[s0:s0 + n, :] * jnp.exp(b_ref - b[s0:s0 + n, :]))
            scores.append(lax.dot_general(qx, kx, nt, preferred_element_type=F32))
        st = st_s[d]
        qe = (q * jnp.exp(b)).astype(BF16)
        kd = (k * jnp.exp(b_last - b)).astype(BF16)
        o = lax.dot_general(qe, st.astype(BF16), nt, preferred_element_type=F32)
        ut = lax.dot_general(v.astype(BF16), kd, tn, preferred_element_type=F32)
        return pairs, scores, v, o, ut, st, b_last

    def dir_values(pairs, scores, v):
        return [jnp.dot(sc.astype(BF16), expand(v[s0:s0 + n, :]), preferred_element_type=F32)
                for (t0, s0, n, ref), sc in zip(pairs, scores)]

    def dir_finish(d, pairs, contribs, o, ut, st, b_last):
        st_s[d] = st * jnp.exp(b_last) + ut * mask_s[...]
        out = [o[j * sub:(j + 1) * sub, :] for j in range(n_sub)]
        for (t0, s0, n, ref), contrib in zip(pairs, contribs):
            for jj in range(n // sub):
                out[t0 // sub + jj] = out[t0 // sub + jj] + contrib[jj * sub:(jj + 1) * sub, :]
        return out

    def chunk_body(i, carry):
        cb = jnp.where(i < n_ctx_chunks, n_ctx_chunks - 1 - i, n_chunks + n_ctx_chunks - 1 - i)
        rf = pl.multiple_of(i * ch, ch)
        rb = pl.multiple_of(cb * ch, ch)
        pf, scf, vf, of, utf, stf, blf = dir_scores(rf, 0)
        pb, scb, vb, ob, utb, stb, blb = dir_scores(rb, 1)
        a = diag_scores(rf)
        cf = dir_values(pf, scf, vf)
        cbw = dir_values(pb, scb, vb)
        o_fwd = dir_finish(0, pf, cf, of, utf, stf, blf)
        o_bwd = dir_finish(1, pb, cbw, ob, utb, stb, blb)
        o_diag = diag_apply(rf, a)
        rows_f = pl.ds(rf, ch)
        o_s[rows_f, :] = o_s[rows_f, :] + jnp.concatenate([x + y for x, y in zip(o_diag, o_fwd)], axis=0)
        rows_b = pl.ds(rb, ch)
        o_s[rows_b, :] = o_s[rows_b, :] + jnp.concatenate(o_bwd, axis=0)
        return carry
    lax.fori_loop(0, n_chunks, chunk_body, 0)

    def out_body(i, carry):
        rows = pl.ds(pl.multiple_of(i * blk, blk), blk)
        o = o_s[rows, :]
        hi, lo = _split_hi_lo(o * o)
        ones = ones_s[...]
        ms = (jnp.dot(hi, ones, preferred_element_type=F32)
              + jnp.dot(lo, ones, preferred_element_type=F32)) * (1.0 / A_HEAD_DIM)
        g = pa_ref[rows, 4 * w:5 * w]
        o_ref[rows, :] = o * lax.rsqrt(ms + RMS_EPS) * ng_ref[...] * (g * jax.nn.sigmoid(g))
        return carry
    lax.fori_loop(0, n_blk, out_body, 0)


def _hgrn(pa, lower_bound, norm_g, nb, lc):
    tt = pa.shape[0]
    w = A_WIDTH
    seq = lambda: pltpu.VMEM((tt, w), F32)
    split = lambda: pltpu.VMEM((w // 128, tt, 128), F32)
    return pl.pallas_call(
        functools.partial(_hgrn_kernel, lc=lc),
        grid=(nb,),
        in_specs=[pl.BlockSpec((tt, A_IN), lambda b: (0, b)),
                  pl.BlockSpec((2, w), lambda b: (0, 0)),
                  pl.BlockSpec((1, w), lambda b: (0, 0))],
        out_specs=pl.BlockSpec((tt, w), lambda b: (0, b)),
        out_shape=jax.ShapeDtypeStruct((tt, nb * w), F32),
        scratch_shapes=[seq(), split(), split(), split(), split(), split(), seq(),
                        pltpu.VMEM((HGRN_CHUNK * HGRN_SUB, w), F32),
                        pltpu.VMEM((2, w, w), F32), pltpu.VMEM((w, w), F32), pltpu.VMEM((w, w), BF16),
                        pltpu.VMEM((2, 128, 128), BF16)],
        compiler_params=_cparams(1, V7X_VMEM_LIMIT),
        name="hgrn2_mixer",
    )(pa, lower_bound, norm_g.reshape(1, w))


def _attn_kernel(sink_ref, q_ref, qs_ref, k_ref, v_ref, o_ref, kz_s, *, lc):
    tt = k_ref.shape[0]
    blk = ATTN_BLOCK
    hd = B_HEAD_DIM
    j = pl.program_id(1)
    n_ctx_blk = lc // blk
    n_blk = tt // blk
    gw = B_GROUP * blk
    nt = (((1,), (1,)), ((), ()))
    tn = (((0,), (0,)), ((), ()))

    @pl.when(j == 0)
    def _():
        lane = lax.broadcasted_iota(jnp.int32, (1, B_KV_WIDTH), 1)
        for g in range(B_KV_HEADS):
            keep = (lane // hd) == g

            def body(i, carry):
                rows = pl.ds(pl.multiple_of(i * blk, blk), blk)
                k = k_ref[rows, :]
                kz_s[g, rows, :] = jnp.where(keep, k, jnp.zeros_like(k))
                return carry
            lax.fori_loop(0, n_blk, body, 0)

    def q_stack(g):
        parts = []
        for hh in range(B_GROUP):
            h = g * B_GROUP + hh
            src = q_ref if (h % 2) == g else qs_ref
            parts.append(src[:, (h // 2) * 128:(h // 2 + 1) * 128])
        return jnp.concatenate(parts, axis=0)

    def scores(g, rows):
        return lax.dot_general(kz_s[g, rows, :], q_stack(g), nt, preferred_element_type=F32)

    def softmax_values(g, pieces):
        sink = jnp.concatenate([jnp.full((1, blk), sink_ref[g * B_GROUP + hh] * LOG2E, F32)
                                for hh in range(B_GROUP)], axis=1)
        m = sink
        for s, _ in pieces:
            m = jnp.maximum(m, jnp.max(s, axis=0, keepdims=True))
        den = jnp.exp2(sink - m)
        acc = None
        for s, vv in pieces:
            p = jnp.exp2(s - m)
            den = den + jnp.sum(p, axis=0, keepdims=True)
            pv = lax.dot_general(vv, p.astype(BF16), tn, preferred_element_type=F32)
            acc = pv if acc is None else acc + pv
        return acc[g * hd:(g + 1) * hd, :] * (1.0 / den)

    def write(outs):
        for pair in range(B_Q_HEADS // 2):
            g, hh = (2 * pair) // B_GROUP, (2 * pair) % B_GROUP
            both = jnp.concatenate([outs[g][:, hh * blk:(hh + 1) * blk],
                                    outs[g][:, (hh + 1) * blk:(hh + 2) * blk]], axis=0)
            o_ref[:, pair * 128:(pair + 1) * 128] = both.T

    @pl.when(j < n_ctx_blk)
    def _():
        sc = [scores(g, pl.ds(0, lc)) for g in range(B_KV_HEADS)]
        write([softmax_values(g, [(sc[g], v_ref[0:lc, :])]) for g in range(B_KV_HEADS)])

    @pl.when(j >= n_ctx_blk)
    def _():
        r_prev = pl.ds(pl.multiple_of((j - 1) * blk, blk), blk)
        r_own = pl.ds(pl.multiple_of(j * blk, blk), blk)
        r_next = pl.ds(pl.multiple_of(jnp.minimum(j + 1, n_blk - 1) * blk, blk), blk)
        r_ctx = pl.ds(0, lc)
        key = lax.broadcasted_iota(jnp.int32, (blk, gw), 0)
        qry = lax.broadcasted_iota(jnp.int32, (blk, gw), 1) % blk
        keep_prev = key >= qry + jnp.where(j > n_ctx_blk, 0, blk)
        keep_next = key <= qry - jnp.where(j < n_blk - 1, 0, blk)
        masked = MASK_VALUE * LOG2E
        sc = [[scores(g, r) for r in (r_prev, r_own, r_next, r_ctx)] for g in range(B_KV_HEADS)]
        outs = []
        for g in range(B_KV_HEADS):
            sp, so, sn, sx = sc[g]
            pieces = [(jnp.where(keep_prev, sp, masked), v_ref[r_prev, :]), (so, v_ref[r_own, :]),
                      (jnp.where(keep_next, sn, masked), v_ref[r_next, :]), (sx, v_ref[r_ctx, :])]
            outs.append(softmax_values(g, pieces))
        write(outs)


def _attn(pq, pqs, pk, pv, sink, nb, lc):
    tt = pq.shape[0]
    blk = ATTN_BLOCK
    return pl.pallas_call(
        functools.partial(_attn_kernel, lc=lc),
        grid=(nb, tt // blk),
        in_specs=[pl.BlockSpec(memory_space=pltpu.SMEM),
                  pl.BlockSpec((blk, B_WIDTH), lambda b, j: (j, b)),
                  pl.BlockSpec((blk, B_WIDTH), lambda b, j: (j, b)),
                  pl.BlockSpec((tt, B_KV_WIDTH), lambda b, j: (0, b)),
                  pl.BlockSpec((tt, B_KV_WIDTH), lambda b, j: (0, b))],
        out_specs=pl.BlockSpec((blk, B_WIDTH), lambda b, j: (j, b)),
        out_shape=jax.ShapeDtypeStruct((tt, nb * B_WIDTH), F32),
        scratch_shapes=[pltpu.VMEM((B_KV_HEADS, tt, B_KV_WIDTH), BF16)],
        compiler_params=_cparams(2),
        name="window_gqa",
    )(sink, pq, pqs, pk, pv)


def _s5_tile_index(d, i, n_ctx, n_all):
    rev = jnp.where(i < n_ctx, n_ctx - 1 - i, n_all + n_ctx - 1 - i)
    return jnp.where(d == 0, i, rev)


def _s5_kernel(u_ref, a_ref, wb_ref, wc_ref, y_ref, u_s, bu_s, y_s, h_s, *, nb):
    d = pl.program_id(0)
    i = pl.program_id(1)
    n = C_NSTATE
    steps = u_ref.shape[0]
    halves = C_WIDTH // 128

    @pl.when(i == 0)
    def _():
        h_s[...] = jnp.zeros(h_s.shape, F32)

    for b in range(nb):
        for c in range(halves):
            lanes = slice(b * C_WIDTH + c * 128, b * C_WIDTH + (c + 1) * 128)
            u_s[c, pl.ds(b, steps, stride=nb), :] = u_ref[:, lanes]
    u = jnp.concatenate([u_s[c] for c in range(halves)], axis=1)
    bu_s[...] = jnp.dot(u.astype(BF16), wb_ref[0], preferred_element_type=F32)
    a_re = jnp.broadcast_to(a_ref[0, 0:1, :], (nb, n))
    a_im = jnp.broadcast_to(a_ref[0, 1:2, :], (nb, n))

    def step(t, carry):
        h_re, h_im = carry
        tt = jnp.where(d == 0, t, steps - 1 - t)
        rows = pl.ds(pl.multiple_of(tt * nb, nb), nb)
        n_re = a_re * h_re - a_im * h_im + bu_s[rows, 0:n]
        n_im = a_re * h_im + a_im * h_re + bu_s[rows, n:2 * n]
        bu_s[rows, 0:n] = n_re
        bu_s[rows, n:2 * n] = n_im
        return n_re, n_im

    h_re, h_im = lax.fori_loop(0, steps, step, (h_s[:, 0:n], h_s[:, n:2 * n]), unroll=4)
    h_s[:, 0:n] = h_re
    h_s[:, n:2 * n] = h_im
    y = jnp.dot(bu_s[...].astype(BF16), wc_ref[0], preferred_element_type=F32)
    for c in range(halves):
        y_s[c] = y[:, c * 128:(c + 1) * 128]
    for b in range(nb):
        for c in range(halves):
            lanes = slice(b * C_WIDTH + c * 128, b * C_WIDTH + (c + 1) * 128)
            y_ref[0, :, lanes] = y_s[c, pl.ds(b, steps, stride=nb), :]


def _s5_scan(pu, a, wb, wc, nb, lc):
    tt = pu.shape[0]
    ts = S5_TILE
    n_all = tt // ts
    n_ctx = lc // ts
    tile = lambda d, i: _s5_tile_index(d, i, n_ctx, n_all)
    flat = lambda width, dt: pltpu.VMEM((ts * nb, width), dt)
    split = lambda: pltpu.VMEM((C_WIDTH // 128, ts * nb, 128), F32)
    return pl.pallas_call(
        functools.partial(_s5_kernel, nb=nb),
        grid=(2, n_all),
        in_specs=[pl.BlockSpec((ts, nb * C_WIDTH), lambda d, i: (tile(d, i), 0)),
                  pl.BlockSpec((1, 2, C_NSTATE), lambda d, i: (d, 0, 0)),
                  pl.BlockSpec((1, C_WIDTH, 2 * C_NSTATE), lambda d, i: (d, 0, 0)),
                  pl.BlockSpec((1, 2 * C_NSTATE, C_WIDTH), lambda d, i: (d, 0, 0))],
        out_specs=pl.BlockSpec((1, ts, nb * C_WIDTH), lambda d, i: (d, tile(d, i), 0)),
        out_shape=jax.ShapeDtypeStruct((2, tt, nb * C_WIDTH), F32),
        scratch_shapes=[split(), flat(2 * C_NSTATE, F32), split(), pltpu.VMEM((nb, 2 * C_NSTATE), F32)],
        compiler_params=_cparams(2, V7X_VMEM_LIMIT),
        name="s5_scan",
    )(pu, a, wb, wc)


def _s5_params(a_re, a_im, log_dt, b_re, b_im, c_re, c_im):
    eye = jnp.eye(C_GROUPS, dtype=F32)
    a_out, wb_out, wc_out = [], [], []
    for k in range(2):
        dt = jnp.exp(log_dt[k])[:, None]
        mag = jnp.exp(a_re[k] * dt)
        ang = a_im[k] * dt
        abar_re, abar_im = mag * jnp.cos(ang), mag * jnp.sin(ang)
        den = a_re[k] * a_re[k] + a_im[k] * a_im[k]
        coef_re = ((abar_re - 1.0) * a_re[k] + abar_im * a_im[k]) / den
        coef_im = (abar_im * a_re[k] - (abar_re - 1.0) * a_im[k]) / den
        bbar_re = coef_re[..., None] * b_re - coef_im[..., None] * b_im
        bbar_im = coef_re[..., None] * b_im + coef_im[..., None] * b_re
        a_out.append(jnp.stack([abar_re.reshape(-1), abar_im.reshape(-1)]))
        drive = lambda bb: jnp.einsum('gpc,gh->gchp', bb, eye).reshape(C_WIDTH, C_NSTATE)
        read = lambda cc: jnp.einsum('gcp,gh->gphc', cc, eye).reshape(C_NSTATE, C_WIDTH)
        wb_out.append(jnp.concatenate([drive(bbar_re), drive(bbar_im)], axis=1))
        wc_out.append(jnp.concatenate([read(c_re[k]), -read(c_im[k])], axis=0))
    return jnp.stack(a_out), jnp.stack(wb_out).astype(BF16), jnp.stack(wc_out).astype(BF16)


def _out_kernel(x_ref, gt_c, gt_b, a_ref, b_ref, u_ref, yf_ref, yb_ref, d_ref, gw_ref, gb_ref, wo_ref, o_ref,
                *, lc):
    y = d_ref[...] * u_ref[...] + yf_ref[0] + yb_ref[0]
    gelu = 0.5 * y * (1.0 + jnp.tanh(math.sqrt(2.0 / math.pi) * (y + 0.044715 * (y * y * y))))
    hg = jnp.dot(gelu.astype(BF16), gw_ref[...], preferred_element_type=F32) + gb_ref[...]
    c = hg[:, :C_WIDTH] * jax.nn.sigmoid(hg[:, C_WIDTH:])
    mix = (jnp.dot(a_ref[...].astype(BF16), wo_ref[0:A_WIDTH, :], preferred_element_type=F32)
           + jnp.dot(b_ref[...].astype(BF16), wo_ref[A_WIDTH:A_WIDTH + B_WIDTH, :], preferred_element_type=F32)
           + jnp.dot(c.astype(BF16), wo_ref[A_WIDTH + B_WIDTH:, :], preferred_element_type=F32))
    x = x_ref[...]
    o_ref[...] = x + jnp.where(_is_ctx_row(x.shape[0], lc), gt_c[0], gt_b[0]) * mix


def _out_proj(X, mod3, a_o, b_o, pu, y2, s5_d, glu_w_s, glu_b, w_out_s, layer, nb, lc):
    tt = X.shape[0]
    d = D_MODEL
    tm = TOKEN_TILE
    const = lambda shape: pl.BlockSpec(shape, lambda i, b: (0,) * len(shape), pipeline_mode=pl.Buffered(1))
    tile = lambda w: pl.BlockSpec((tm, w), lambda i, b: (i, b))
    return pl.pallas_call(
        functools.partial(_out_kernel, lc=lc),
        grid=(tt // tm, nb),
        in_specs=[tile(d)] + _mod_specs(nb, (2,)) + [
                  tile(A_WIDTH), tile(B_WIDTH), tile(C_WIDTH),
                  pl.BlockSpec((1, tm, C_WIDTH), lambda i, b: (0, i, b)),
                  pl.BlockSpec((1, tm, C_WIDTH), lambda i, b: (1, i, b)),
                  const((1, C_WIDTH)), _layer_weight(glu_w_s, (layer,)), const((1, 2 * C_WIDTH)),
                  _layer_weight(w_out_s, (layer,))],
        out_specs=tile(d),
        out_shape=jax.ShapeDtypeStruct(X.shape, F32),
        input_output_aliases={0: 0},
        compiler_params=_cparams(2),
        name="out_proj",
    )(X, mod3, mod3, a_o, b_o, pu, y2, y2, s5_d.reshape(1, C_WIDTH), glu_w_s, glu_b.reshape(1, 2 * C_WIDTH), w_out_s)


def _final_kernel(x_ref, g_ref, o_ref):
    x = x_ref[...]
    o_ref[0] = x * lax.rsqrt(jnp.mean(x * x, axis=-1, keepdims=True) + RMS_EPS) * g_ref[...]


def _final_norm(X, g, nb, lc):
    tt = X.shape[0]
    d = D_MODEL
    tm = ROW_TILE
    nct = lc // tm
    return pl.pallas_call(
        _final_kernel,
        grid=(nb, (tt - lc) // tm),
        in_specs=[pl.BlockSpec((tm, d), lambda b, i: (i + nct, b)),
                  pl.BlockSpec((1, d), lambda b, i: (0, 0))],
        out_specs=pl.BlockSpec((1, tm, d), lambda b, i: (b, i, 0)),
        out_shape=jax.ShapeDtypeStruct((nb, tt - lc, d), F32),
        compiler_params=_cparams(2),
        name="final_norm",
    )(X, g.reshape(1, d))


def _rope_tables(length, lc):
    t = jnp.arange(length)
    pos = jnp.stack([(t // GRID_W).astype(F32), (t % GRID_W).astype(F32)], axis=1)
    inv_freq = ROPE_BASE ** (-jnp.arange(ROPE_PAIRS, dtype=F32) / ROPE_PAIRS)
    ang = pos[:, :, None] * inv_freq
    cos = jnp.cos(ang)[:, :, None, :]
    sin = jnp.sin(ang)[:, :, None, :]
    cos = jnp.broadcast_to(cos, (length, 2, 2, ROPE_PAIRS)).reshape(length, B_HEAD_DIM)
    sin = jnp.concatenate([-sin, sin], axis=2).reshape(length, B_HEAD_DIM)
    cos = jnp.concatenate([jnp.ones((lc, B_HEAD_DIM), F32), cos], axis=0)
    sin = jnp.concatenate([jnp.zeros((lc, B_HEAD_DIM), F32), sin], axis=0)
    return jnp.tile(cos, (1, 128 // B_HEAD_DIM)), jnp.tile(sin, (1, 128 // B_HEAD_DIM))


def kernel(x, c, ctx, c_ctx, ada_w, ada_b, norm_g, ffn_w1, ffn_w2, w_in, w_out, hgrn_lower_bounds,
           hgrn_norm_g, attn_sink, s5_a_re, s5_a_im, s5_log_dt, s5_b_re, s5_b_im, s5_c_re, s5_c_im,
           s5_d, s5_glu_w, s5_glu_b, final_norm_g):
    nb, length, d = x.shape
    lc = ctx.shape[1]
    tt = lc + length
    depth = ada_w.shape[0]
    assert d == D_MODEL and nb == 8 and lc % ROW_TILE == 0 and length % ROW_TILE == 0 and tt % TOKEN_TILE == 0

    act = jnp.concatenate([c, c_ctx[None], jnp.zeros((16 - nb - 1, d), F32)], axis=0)
    mod = _modulation(act, ada_w, ada_b).reshape(depth, 16, 3, 3, d)[:, :nb + 1]
    cos_t, sin_t = _rope_tables(length, lc)
    lb_soft = jax.nn.softmax(hgrn_lower_bounds.astype(F32), axis=0)
    lower_bound = jnp.cumsum(lb_soft, axis=0) - lb_soft[0]

    w1s, w2s = ffn_w1.astype(BF16), ffn_w2.astype(BF16)
    w_in_s, w_out_s, glu_w_s = w_in.astype(BF16), w_out.astype(BF16), s5_glu_w.astype(BF16)
    X = (ctx, x)
    for l in range(depth):
        mod3 = [mod[l, :, i].reshape((nb + 1) * 3, 1, d) for i in range(3)]
        X = _ffn(X, mod3[0], norm_g[l, 0], w1s, w2s, (l, 0), nb, lc)
        pa, pq, pqs, pk, pv, pu = _proj(X, mod3[1], norm_g[l, 1], w_in_s, l, cos_t, sin_t, nb, lc)
        a_o = _hgrn(pa, lower_bound[l], hgrn_norm_g[l], nb, lc)
        b_o = _attn(pq, pqs, pk, pv, attn_sink[l], nb, lc)
        s5a, s5wb, s5wc = _s5_params(s5_a_re[l], s5_a_im[l], s5_log_dt[l], s5_b_re[l], s5_b_im[l],
                                     s5_c_re[l], s5_c_im[l])
        y2 = _s5_scan(pu, s5a, s5wb, s5wc, nb, lc)
        X = _out_proj(X, mod3[1], a_o, b_o, pu, y2, s5_d[l], glu_w_s, s5_glu_b[l], w_out_s, l, nb, lc)
        X = _ffn(X, mod3[2], norm_g[l, 2], w1s, w2s, (l, 1), nb, lc)
    return _final_norm(X, final_norm_g, nb, lc)
```

```python
import functools
import math

import jax
import jax.numpy as jnp
from jax import lax
from jax.experimental import pallas as pl
from jax.experimental.pallas import tpu as pltpu

F32 = jnp.float32
BF16 = jnp.bfloat16

D_MODEL = 1024
DEPTH = 4
GRID_W = 64
RMS_EPS = 1e-6
N_MOD = 9
D_FF = 2816
A_HEADS = 4
A_HEAD_DIM = 64
A_WIDTH = A_HEADS * A_HEAD_DIM
GLR_CHUNK = 16
B_Q_HEADS = 8
B_KV_HEADS = 2
B_GROUP = B_Q_HEADS // B_KV_HEADS
B_HEAD_DIM = 64
B_WIDTH = B_Q_HEADS * B_HEAD_DIM
B_KV_WIDTH = B_KV_HEADS * B_HEAD_DIM
WINDOW = 128
ATTN_BLOCK = 128
ROPE_BASE = 10000.0
ROPE_PAIRS = B_HEAD_DIM // 4
MASK_VALUE = -1e9
LOG2E = math.log2(math.e)
C_GROUPS = 16
C_GROUP_CH = 16
C_WIDTH = C_GROUPS * C_GROUP_CH
C_STATE = 64
C_NSTATE = C_GROUPS * C_STATE
D_MIX = A_WIDTH + B_WIDTH + C_WIDTH
A_IN = 5 * A_WIDTH
D_IN = A_IN + B_WIDTH + 2 * B_KV_WIDTH + C_WIDTH

V7X_VMEM_LIMIT = 56 * 1024 * 1024
ROW_TILE = 256
TOKEN_TILE = 576
S5_TILE = 128
S5_GROUP = 16
HGRN_CHUNK = 128
HGRN_SUB = 8


def _cparams(n_axes, vmem=None):
    return pltpu.CompilerParams(dimension_semantics=("arbitrary",) * n_axes, vmem_limit_bytes=vmem)


def _split_hi_lo(x):
    hi = x.astype(BF16)
    lo = (x - hi.astype(F32)).astype(BF16)
    return hi, lo


def _mod_kernel(act_ref, w_ref, b_ref, o_ref):
    a = act_ref[...]
    a = (a * jax.nn.sigmoid(a)).astype(BF16)
    o_ref[0] = jnp.dot(a, w_ref[0].astype(BF16), preferred_element_type=F32) + b_ref[0]


def _modulation(act, ada_w, ada_b):
    depth, d, n = ada_w.shape
    tn = 1024
    return pl.pallas_call(
        _mod_kernel,
        grid=(depth, n // tn),
        in_specs=[pl.BlockSpec((16, d), lambda l, j: (0, 0)),
                  pl.BlockSpec((1, d, tn), lambda l, j: (l, 0, j)),
                  pl.BlockSpec((1, 1, tn), lambda l, j: (l, 0, j))],
        out_specs=pl.BlockSpec((1, 16, tn), lambda l, j: (l, 0, j)),
        out_shape=jax.ShapeDtypeStruct((depth, 16, n), F32),
        compiler_params=_cparams(2),
        name="adaln_mod",
    )(act, ada_w, ada_b.reshape(depth, 1, n))


def _is_ctx_row(tm, lc, first_tile=0):
    return lax.broadcasted_iota(jnp.int32, (tm, D_MODEL), 0) < lc - (pl.program_id(0) + first_tile) * tm


def _adaln(x, g, is_ctx, sh_c, sh_b, sc_c, sc_b):
    r = lax.rsqrt(jnp.mean(x * x, axis=-1, keepdims=True) + RMS_EPS)
    gain = jnp.where(is_ctx, g * (1.0 + sc_c[0]), g * (1.0 + sc_b[0]))
    return (x * r) * gain + jnp.where(is_ctx, sh_c[0], sh_b[0])


def _mod_specs(nb, base, comps):
    specs = []
    for comp in comps:
        specs.append(pl.BlockSpec((1, 1, D_MODEL), lambda i, b, comp=comp: (base + nb * 3 + comp, 0, 0)))
        specs.append(pl.BlockSpec((1, 1, D_MODEL), lambda i, b, comp=comp: (base + b * 3 + comp, 0, 0)))
    return specs


def _row_spec(table, row):
    return pl.BlockSpec((1, 1, table.shape[-1]), lambda i, b: (row, 0, 0))


def _ffn_rows(x, is_ctx, sh_c, sh_b, sc_c, sc_b, gt_c, gt_b, g_ref, w1_ref, w2_ref):
    h = _adaln(x, g_ref[0], is_ctx, sh_c, sh_b, sc_c, sc_b).astype(BF16)
    gu = jnp.dot(h, w1_ref[...], preferred_element_type=F32)
    gate = gu[:, :D_FF]
    a = (gate * jax.nn.sigmoid(gate) * gu[:, D_FF:]).astype(BF16)
    y = jnp.dot(a, w2_ref[...], preferred_element_type=F32)
    return x + jnp.where(is_ctx, 0.5 * gt_c[0], 0.5 * gt_b[0]) * y


def _ffn_kernel(x_ref, *refs, lc):
    x = x_ref[...]
    refs[-1][...] = _ffn_rows(x, _is_ctx_row(x.shape[0], lc), *refs[:-1])


def _ffn_entry_kernel(ctx_ref, lat_ref, *refs, lc):
    x = jnp.where(pl.program_id(0) * ctx_ref.shape[1] < lc, ctx_ref[0], lat_ref[0])
    refs[-1][...] = _ffn_rows(x, _is_ctx_row(x.shape[0], lc), *refs[:-1])


def _ffn_exit_kernel(x_ref, *refs, lc):
    x = x_ref[...]
    fg_ref, o_ref = refs[-2:]
    y = _ffn_rows(x, _is_ctx_row(x.shape[0], lc, lc // x.shape[0]), *refs[:-2])
    o_ref[0] = y * lax.rsqrt(jnp.mean(y * y, axis=-1, keepdims=True) + RMS_EPS) * fg_ref[...]


def _layer_weight(stack, index):
    lead = len(index)
    return pl.BlockSpec((None,) * lead + stack.shape[lead:], lambda i, b: tuple(index) + (0, 0),
                        pipeline_mode=pl.Buffered(1))


def _ffn(X, mod, base, norm_g, g_row, w1s, w2s, index, nb, lc, final_g=None):
    d = D_MODEL
    rest = _mod_specs(nb, base, (0, 1, 2)) + [_row_spec(norm_g, g_row),
                                              _layer_weight(w1s, index), _layer_weight(w2s, index)]
    stream_tile = lambda tm: pl.BlockSpec((tm, d), lambda i, b: (i, b))
    extra, alias = (), {}
    if isinstance(X, tuple):
        ctx, lat = X
        tm = ROW_TILE
        nct = lc // tm
        tt = lc + lat.shape[1]
        body = functools.partial(_ffn_entry_kernel, lc=lc)
        srcs = [pl.BlockSpec((1, tm, d), lambda i, b: (b, jnp.minimum(i, nct - 1), 0)),
                pl.BlockSpec((1, tm, d), lambda i, b: (b, jnp.maximum(i - nct, 0), 0))]
        args = (ctx, lat)
        n_tiles, out_spec, out_shape = tt // tm, stream_tile(tm), (tt, nb * d)
    elif final_g is not None:
        tm = ROW_TILE
        nct = lc // tm
        tt = X.shape[0]
        body = functools.partial(_ffn_exit_kernel, lc=lc)
        srcs, args = [pl.BlockSpec((tm, d), lambda i, b: (i + nct, b))], (X,)
        rest = rest + [pl.BlockSpec((1, d), lambda i, b: (0, 0))]
        extra = (final_g.reshape(1, d),)
        n_tiles, out_spec, out_shape = (tt - lc) // tm, pl.BlockSpec((1, tm, d), lambda i, b: (b, i, 0)), (nb, tt - lc, d)
    else:
        tm = TOKEN_TILE
        tt = X.shape[0]
        body = functools.partial(_ffn_kernel, lc=lc)
        srcs, args, alias = [stream_tile(tm)], (X,), {0: 0}
        n_tiles, out_spec, out_shape = tt // tm, stream_tile(tm), (tt, nb * d)
    return pl.pallas_call(
        body,
        grid=(n_tiles, nb),
        in_specs=srcs + rest,
        out_specs=out_spec,
        out_shape=jax.ShapeDtypeStruct(out_shape, F32),
        input_output_aliases=alias,
        compiler_params=_cparams(2, V7X_VMEM_LIMIT),
        name="ffn_half_step",
    )(*args, *([mod] * 6), norm_g, w1s, w2s, *extra)


def _rope(x, cos, sin, first_half):
    w = x.shape[-1]
    nxt = pltpu.roll(x, w - ROPE_PAIRS, axis=1)
    prv = pltpu.roll(x, ROPE_PAIRS, axis=1)
    return x * cos + jnp.where(first_half, nxt, prv) * sin


def _proj_kernel(x_ref, sh_c, sh_b, sc_c, sc_b, g_ref, w_ref, cos_ref, sin_ref,
                 pa_ref, pq_ref, pqs_ref, pk_ref, pv_ref, pu_ref, *, lc):
    x = x_ref[...]
    h = _adaln(x, g_ref[0], _is_ctx_row(x.shape[0], lc), sh_c, sh_b, sc_c, sc_b).astype(BF16)
    p = jnp.dot(h, w_ref[...], preferred_element_type=F32)
    o = A_IN
    pa_ref[...] = p[:, :o]
    cos = cos_ref[...]
    sin = sin_ref[...]
    lane = lax.broadcasted_iota(jnp.int32, (1, B_WIDTH), 1)
    first_half = (lane % (2 * ROPE_PAIRS)) < ROPE_PAIRS
    cos_q = jnp.concatenate([cos] * (B_WIDTH // 128), axis=1)
    sin_q = jnp.concatenate([sin] * (B_WIDTH // 128), axis=1)
    q = _rope(p[:, o:o + B_WIDTH], cos_q, sin_q, first_half)
    q = q * (B_HEAD_DIM ** -0.5 * LOG2E)
    pq_ref[...] = q.astype(BF16)
    low_head = (lane % 128) < B_HEAD_DIM
    q_sw = jnp.where(low_head, pltpu.roll(q, B_WIDTH - B_HEAD_DIM, axis=1), pltpu.roll(q, B_HEAD_DIM, axis=1))
    pqs_ref[...] = q_sw.astype(BF16)
    o += B_WIDTH
    k = _rope(p[:, o:o + B_KV_WIDTH], cos, sin, first_half[:, :B_KV_WIDTH])
    pk_ref[...] = k.astype(BF16)
    o += B_KV_WIDTH
    pv_ref[...] = p[:, o:o + B_KV_WIDTH].astype(BF16)
    o += B_KV_WIDTH
    pu_ref[...] = p[:, o:o + C_WIDTH]


def _proj(X, mod, base, norm_g, g_row, w_in_s, layer, cos_t, sin_t, nb, lc):
    tt = X.shape[0]
    d = D_MODEL
    tm = TOKEN_TILE
    const = lambda shape: pl.BlockSpec(shape, lambda i, b: (0,) * len(shape), pipeline_mode=pl.Buffered(1))
    tile = lambda w: pl.BlockSpec((tm, w), lambda i, b: (i, b))
    widths = (A_IN, B_WIDTH, B_WIDTH, B_KV_WIDTH, B_KV_WIDTH, C_WIDTH)
    dtypes = (F32, BF16, BF16, BF16, BF16, F32)
    return pl.pallas_call(
        functools.partial(_proj_kernel, lc=lc),
        grid=(tt // tm, nb),
        in_specs=[tile(d)] + _mod_specs(nb, base, (0, 1)) + [
                  _row_spec(norm_g, g_row), _layer_weight(w_in_s, (layer,)),
                  pl.BlockSpec((tm, 128), lambda i, b: (i, 0)), pl.BlockSpec((tm, 128), lambda i, b: (i, 0))],
        out_specs=[tile(w) for w in widths],
        out_shape=[jax.ShapeDtypeStruct((tt, nb * w), dt) for w, dt in zip(widths, dtypes)],
        compiler_params=_cparams(2, V7X_VMEM_LIMIT),
        name="in_proj",
    )(X, mod, mod, mod, mod, norm_g, w_in_s, cos_t, sin_t)


def _hgrn_pairs(lo, hi, reverse):
    if hi - lo <= HGRN_SUB:
        return []
    mid = (lo + hi) // 2
    here = (lo, mid, mid - lo, mid) if reverse else (mid, lo, mid - lo, mid - 1)
    return [here] + _hgrn_pairs(lo, mid, reverse) + _hgrn_pairs(mid, hi, reverse)


def _hgrn_kernel(pa_ref, lb_ref, ng_ref, o_ref,
                 qs_s, kf_s, kb_s, bf_s, bb_s, v_s, o_s, p_s, st_s, mask_s, ones_s, tri_s, *, lc):
    tt = pa_ref.shape[0]
    w = A_WIDTH
    ch = HGRN_CHUNK
    sub = HGRN_SUB
    n_sub = ch // sub
    blk = 128
    n_blk = tt // blk
    n_chunks = tt // ch
    n_ctx_chunks = lc // ch
    nt = (((1,), (1,)), ((), ()))
    tn = (((0,), (0,)), ((), ()))

    row = lax.broadcasted_iota(jnp.int32, (w, w), 0)
    col = lax.broadcasted_iota(jnp.int32, (w, w), 1)
    same_head = jnp.where((row // A_HEAD_DIM) == (col // A_HEAD_DIM), 1.0, 0.0)
    mask_s[...] = same_head
    ones_s[...] = same_head.astype(BF16)
    lane = lax.broadcasted_iota(jnp.int32, (1, w), 1)
    head_mask = [jnp.where(lane // A_HEAD_DIM == h, 1.0, 0.0) for h in range(A_HEADS)]
    r = lax.broadcasted_iota(jnp.int32, (blk, blk), 0)
    c = lax.broadcasted_iota(jnp.int32, (blk, blk), 1)
    same_chunk = (r // ch) == (c // ch)
    tri_s[0] = jnp.where(same_chunk, jnp.where(c <= r, 1.0, 0.0), 0.0).astype(BF16)
    tri_s[1] = jnp.where(same_chunk, jnp.where(c >= r, 1.0, 0.0), 0.0).astype(BF16)
    st_s[...] = jnp.zeros(st_s.shape, F32)

    def chunk_cumsum(logf, tri):
        hi = logf.astype(BF16)
        rem = logf - hi.astype(F32)
        mid = rem.astype(BF16)
        lo = (rem - mid.astype(F32)).astype(BF16)
        return (jnp.dot(tri, hi, preferred_element_type=F32)
                + jnp.dot(tri, mid, preferred_element_type=F32)
                + jnp.dot(tri, lo, preferred_element_type=F32))

    def load(ref, rows):
        return jnp.concatenate([ref[c, rows, :] for c in range(w // 128)], axis=1)

    def store(ref, rows, val):
        for c in range(w // 128):
            ref[c, rows, :] = val[:, c * 128:(c + 1) * 128]

    def load_row(ref, r):
        return load(ref, pl.ds(r, sub, stride=0))

    def prep_body(i, carry):
        rows = pl.ds(pl.multiple_of(i * blk, blk), blk)
        q = pa_ref[rows, 0:w]
        qs_s[rows, :] = q * jax.nn.sigmoid(q)
        store(v_s, rows, pa_ref[rows, w:2 * w])
        o_s[rows, :] = jnp.zeros((blk, w), F32)
        for d, (k_s, b_s) in enumerate(((kf_s, bf_s), (kb_s, bb_s))):
            lb = lb_ref[d:d + 1, :]
            z = pa_ref[rows, (2 + d) * w:(3 + d) * w]
            t = jnp.exp(-jnp.abs(z))
            big = 1.0 / (1.0 + t)
            small = t * big
            f = lb + (1.0 - lb) * jnp.where(z >= 0, big, small)
            store(k_s, rows, (1.0 - lb) * jnp.where(z >= 0, small, big))
            store(b_s, rows, chunk_cumsum(jnp.log(f), tri_s[d]))
        return carry
    lax.fori_loop(0, n_blk, prep_body, 0)

    def expand(x):
        return jnp.concatenate([x * head_mask[h] for h in range(A_HEADS)], axis=0).astype(BF16)

    def diag_scores(r0):
        t_i = lax.broadcasted_iota(jnp.int32, (sub, w), 0)
        for j in range(n_sub):
            base = r0 + j * sub
            rows = pl.ds(base, sub)
            q = qs_s[rows, :]
            bf = load(bf_s, rows)
            bb = load(bb_s, rows)
            for s in range(sub):
                kf_row = load_row(kf_s, base + s)
                kb_row = load_row(kb_s, base + s)
                arg = jnp.where(t_i >= s, bf - load_row(bf_s, base + s), bb - load_row(bb_s, base + s))
                kk = jnp.where(t_i > s, kf_row, jnp.where(t_i < s, kb_row, kf_row + kb_row))
                p_s[(j * sub + s) * sub:(j * sub + s + 1) * sub, :] = q * jnp.exp(arg) * kk
        return jnp.dot(p_s[...].astype(BF16), ones_s[...], preferred_element_type=F32)

    def diag_apply(r0, a):
        out = []
        for j in range(n_sub):
            acc = None
            for s in range(sub):
                v_row = load_row(v_s, r0 + j * sub + s)
                term = a[(j * sub + s) * sub:(j * sub + s + 1) * sub, :] * v_row
                acc = term if acc is None else acc + term
            out.append(acc)
        return out

    def dir_scores(r0, d):
        k_s, b_s = (kf_s, bf_s) if d == 0 else (kb_s, bb_s)
        rows = pl.ds(r0, ch)
        b = load(b_s, rows)
        q = qs_s[rows, :]
        k = load(k_s, rows)
        v = load(v_s, rows)
        last = (ch - 1) if d == 0 else 0
        b_last = b[last:last + 1, :]
        pairs = _hgrn_pairs(0, ch, d == 1)
        scores = []
        for t0, s0, n, ref in pairs:
            b_ref = b[ref:ref + 1, :]
            qx = (q[t0:t0 + n, :] * jnp.exp(b[t0:t0 + n, :] - b_ref)).astype(BF16)
            kx = expand(k[s0:s0 + n, :] * jnp.exp(b_ref - b[s0:s0 + n, :]))
            scores.append(lax.dot_general(qx, kx, nt, preferred_element_type=F32))
        st = st_s[d]
        qe = (q * jnp.exp(b)).astype(BF16)
        kd = (k * jnp.exp(b_last - b)).astype(BF16)
        o = lax.dot_general(qe, st.astype(BF16), nt, preferred_element_type=F32)
        ut = lax.dot_general(v.astype(BF16), kd, tn, preferred_element_type=F32)
        return pairs, scores, v, o, ut, st, b_last

    def dir_values(pairs, scores, v):
        return [jnp.dot(sc.astype(BF16), expand(v[s0:s0 + n, :]), preferred_element_type=F32)
                for (t0, s0, n, ref), sc in zip(pairs, scores)]

    def dir_finish(d, pairs, contribs, o, ut, st, b_last):
        st_s[d] = st * jnp.exp(b_last) + ut * mask_s[...]
        out = [o[j * sub:(j + 1) * sub, :] for j in range(n_sub)]
        for (t0, s0, n, ref), contrib in zip(pairs, contribs):
            for jj in range(n // sub):
                out[t0 // sub + jj] = out[t0 // sub + jj] + contrib[jj * sub:(jj + 1) * sub, :]
        return out

    def chunk_body(i, carry):
        cb = jnp.where(i < n_ctx_chunks, n_ctx_chunks - 1 - i, n_chunks + n_ctx_chunks - 1 - i)
        rf = pl.multiple_of(i * ch, ch)
        rb = pl.multiple_of(cb * ch, ch)
        pf, scf, vf, of, utf, stf, blf = dir_scores(rf, 0)
        pb, scb, vb, ob, utb, stb, blb = dir_scores(rb, 1)
        a = diag_scores(rf)
        cf = dir_values(pf, scf, vf)
        cbw = dir_values(pb, scb, vb)
        o_fwd = dir_finish(0, pf, cf, of, utf, stf, blf)
        o_bwd = dir_finish(1, pb, cbw, ob, utb, stb, blb)
        o_diag = diag_apply(rf, a)
        rows_f = pl.ds(rf, ch)
        o_s[rows_f, :] = o_s[rows_f, :] + jnp.concatenate([x + y for x, y in zip(o_diag, o_fwd)], axis=0)
        rows_b = pl.ds(rb, ch)
        o_s[rows_b, :] = o_s[rows_b, :] + jnp.concatenate(o_bwd, axis=0)
        return carry
    lax.fori_loop(0, n_chunks, chunk_body, 0)

    def out_body(i, carry):
        rows = pl.ds(pl.multiple_of(i * blk, blk), blk)
        o = o_s[rows, :]
        hi, lo = _split_hi_lo(o * o)
        ones = ones_s[...]
        ms = (jnp.dot(hi, ones, preferred_element_type=F32)
              + jnp.dot(lo, ones, preferred_element_type=F32)) * (1.0 / A_HEAD_DIM)
        g = pa_ref[rows, 4 * w:5 * w]
        o_ref[rows, :] = o * lax.rsqrt(ms + RMS_EPS) * ng_ref[...] * (g * jax.nn.sigmoid(g))
        return carry
    lax.fori_loop(0, n_blk, out_body, 0)


def _hgrn(pa, lower_bound, norm_g, nb, lc):
    tt = pa.shape[0]
    w = A_WIDTH
    seq = lambda: pltpu.VMEM((tt, w), F32)
    split = lambda: pltpu.VMEM((w // 128, tt, 128), F32)
    return pl.pallas_call(
        functools.partial(_hgrn_kernel, lc=lc),
        grid=(nb,),
        in_specs=[pl.BlockSpec((tt, A_IN), lambda b: (0, b)),
                  pl.BlockSpec((2, w), lambda b: (0, 0)),
                  pl.BlockSpec((1, w), lambda b: (0, 0))],
        out_specs=pl.BlockSpec((tt, w), lambda b: (0, b)),
        out_shape=jax.ShapeDtypeStruct((tt, nb * w), F32),
        scratch_shapes=[seq(), split(), split(), split(), split(), split(), seq(),
                        pltpu.VMEM((HGRN_CHUNK * HGRN_SUB, w), F32),
                        pltpu.VMEM((2, w, w), F32), pltpu.VMEM((w, w), F32), pltpu.VMEM((w, w), BF16),
                        pltpu.VMEM((2, 128, 128), BF16)],
        compiler_params=_cparams(1, V7X_VMEM_LIMIT),
        name="hgrn2_mixer",
    )(pa, lower_bound, norm_g.reshape(1, w))


def _attn_kernel(sink_ref, q_ref, qs_ref, k_ref, v_ref, o_ref, kz_s, *, lc):
    tt = k_ref.shape[0]
    blk = ATTN_BLOCK
    hd = B_HEAD_DIM
    j = pl.program_id(1)
    n_ctx_blk = lc // blk
    n_blk = tt // blk
    gw = B_GROUP * blk
    nt = (((1,), (1,)), ((), ()))
    tn = (((0,), (0,)), ((), ()))

    @pl.when(j == 0)
    def _():
        lane = lax.broadcasted_iota(jnp.int32, (1, B_KV_WIDTH), 1)
        for g in range(B_KV_HEADS):
            keep = (lane // hd) == g

            def body(i, carry):
                rows = pl.ds(pl.multiple_of(i * blk, blk), blk)
                k = k_ref[rows, :]
                kz_s[g, rows, :] = jnp.where(keep, k, jnp.zeros_like(k))
                return carry
            lax.fori_loop(0, n_blk, body, 0)

    def q_stack(g):
        parts = []
        for hh in range(B_GROUP):
            h = g * B_GROUP + hh
            src = q_ref if (h % 2) == g else qs_ref
            parts.append(src[:, (h // 2) * 128:(h // 2 + 1) * 128])
        return jnp.concatenate(parts, axis=0)

    def scores(g, rows):
        return lax.dot_general(kz_s[g, rows, :], q_stack(g), nt, preferred_element_type=F32)

    def softmax_values(g, pieces):
        sink = jnp.concatenate([jnp.full((1, blk), sink_ref[g * B_GROUP + hh] * LOG2E, F32)
                                for hh in range(B_GROUP)], axis=1)
        m = sink
        for s, _ in pieces:
            m = jnp.maximum(m, jnp.max(s, axis=0, keepdims=True))
        den = jnp.exp2(sink - m)
        acc = None
        for s, vv in pieces:
            p = jnp.exp2(s - m)
            den = den + jnp.sum(p, axis=0, keepdims=True)
            pv = lax.dot_general(vv, p.astype(BF16), tn, preferred_element_type=F32)
            acc = pv if acc is None else acc + pv
        return acc[g * hd:(g + 1) * hd, :] * (1.0 / den)

    def write(outs):
        for pair in range(B_Q_HEADS // 2):
            g, hh = (2 * pair) // B_GROUP, (2 * pair) % B_GROUP
            both = jnp.concatenate([outs[g][:, hh * blk:(hh + 1) * blk],
                                    outs[g][:, (hh + 1) * blk:(hh + 2) * blk]], axis=0)
            o_ref[:, pair * 128:(pair + 1) * 128] = both.T

    @pl.when(j < n_ctx_blk)
    def _():
        sc = [scores(g, pl.ds(0, lc)) for g in range(B_KV_HEADS)]
        write([softmax_values(g, [(sc[g], v_ref[0:lc, :])]) for g in range(B_KV_HEADS)])

    @pl.when(j >= n_ctx_blk)
    def _():
        r_prev = pl.ds(pl.multiple_of((j - 1) * blk, blk), blk)
        r_own = pl.ds(pl.multiple_of(j * blk, blk), blk)
        r_next = pl.ds(pl.multiple_of(jnp.minimum(j + 1, n_blk - 1) * blk, blk), blk)
        r_ctx = pl.ds(0, lc)
        key = lax.broadcasted_iota(jnp.int32, (blk, gw), 0)
        qry = lax.broadcasted_iota(jnp.int32, (blk, gw), 1) % blk
        keep_prev = key >= qry + jnp.where(j > n_ctx_blk, 0, blk)
        keep_next = key <= qry - jnp.where(j < n_blk - 1, 0, blk)
        masked = MASK_VALUE * LOG2E
        sc = [[scores(g, r) for r in (r_prev, r_own, r_next, r_ctx)] for g in range(B_KV_HEADS)]
        outs = []
        for g in range(B_KV_HEADS):
            sp, so, sn, sx = sc[g]
            pieces = [(jnp.where(keep_prev, sp, masked), v_ref[r_prev, :]), (so, v_ref[r_own, :]),
                      (jnp.where(keep_next, sn, masked), v_ref[r_next, :]), (sx, v_ref[r_ctx, :])]
            outs.append(softmax_values(g, pieces))
        write(outs)


def _attn(pq, pqs, pk, pv, sink, nb, lc):
    tt = pq.shape[0]
    blk = ATTN_BLOCK
    return pl.pallas_call(
        functools.partial(_attn_kernel, lc=lc),
        grid=(nb, tt // blk),
        in_specs=[pl.BlockSpec(memory_space=pltpu.SMEM),
                  pl.BlockSpec((blk, B_WIDTH), lambda b, j: (j, b)),
                  pl.BlockSpec((blk, B_WIDTH), lambda b, j: (j, b)),
                  pl.BlockSpec((tt, B_KV_WIDTH), lambda b, j: (0, b)),
                  pl.BlockSpec((tt, B_KV_WIDTH), lambda b, j: (0, b))],
        out_specs=pl.BlockSpec((blk, B_WIDTH), lambda b, j: (j, b)),
        out_shape=jax.ShapeDtypeStruct((tt, nb * B_WIDTH), F32),
        scratch_shapes=[pltpu.VMEM((B_KV_HEADS, tt, B_KV_WIDTH), BF16)],
        compiler_params=_cparams(2),
        name="window_gqa",
    )(sink, pq, pqs, pk, pv)


def _s5_tile_index(reverse, i, n_ctx, n_all):
    if not reverse:
        return i
    return jnp.where(i < n_ctx, n_ctx - 1 - i, n_all + n_ctx - 1 - i)


def _s5_kernel(u_ref, a_ref, wb_ref, wc_ref, y_ref, u_s, y_s, h_s, *, nb, reverse):
    i = pl.program_id(0)
    n = C_NSTATE
    steps = u_ref.shape[0]
    halves = C_WIDTH // 128
    group = S5_GROUP
    n_groups = steps // group

    @pl.when(i == 0)
    def _():
        h_s[...] = jnp.zeros(h_s.shape, F32)

    for b in range(nb):
        for c in range(halves):
            lanes = slice(b * C_WIDTH + c * 128, b * C_WIDTH + (c + 1) * 128)
            u_s[c, pl.ds(b, steps, stride=nb), :] = u_ref[:, lanes]
    a_re = jnp.broadcast_to(a_ref[0:1, :], (nb, n))
    a_im = jnp.broadcast_to(a_ref[1:2, :], (nb, n))

    def drive(g):
        rows = slice(g * group * nb, (g + 1) * group * nb)
        u = jnp.concatenate([u_s[c, rows, :] for c in range(halves)], axis=1)
        return jnp.dot(u.astype(BF16), wb_ref[...], preferred_element_type=F32)

    order = list(range(n_groups))[::-1] if reverse else list(range(n_groups))
    lookahead = 2
    driven = {g: drive(g) for g in order[:lookahead]}
    h_re, h_im = h_s[:, 0:n], h_s[:, n:2 * n]
    for pos, g in enumerate(order):
        bu = driven.pop(g)
        states = [None] * group
        for t in (range(group - 1, -1, -1) if reverse else range(group)):
            slab = bu[t * nb:(t + 1) * nb, :]
            h_re, h_im = (a_re * h_re - a_im * h_im + slab[:, 0:n],
                          a_re * h_im + a_im * h_re + slab[:, n:2 * n])
            states[t] = jnp.concatenate([h_re, h_im], axis=1)
        if pos + lookahead < n_groups:
            nxt = order[pos + lookahead]
            driven[nxt] = drive(nxt)
        hb = jnp.concatenate(states, axis=0).astype(BF16)
        y = jnp.dot(hb, wc_ref[...], preferred_element_type=F32)
        rows = slice(g * group * nb, (g + 1) * group * nb)
        for c in range(halves):
            y_s[c, rows, :] = y[:, c * 128:(c + 1) * 128]
    h_s[:, 0:n] = h_re
    h_s[:, n:2 * n] = h_im
    for b in range(nb):
        for c in range(halves):
            lanes = slice(b * C_WIDTH + c * 128, b * C_WIDTH + (c + 1) * 128)
            y_ref[:, lanes] = y_s[c, pl.ds(b, steps, stride=nb), :]


def _s5_scan(pu, a, wb, wc, index, nb, lc, reverse):
    tt = pu.shape[0]
    ts = S5_TILE
    n_all = tt // ts
    n_ctx = lc // ts
    tile = lambda i: (_s5_tile_index(reverse, i, n_ctx, n_all), 0)
    split = lambda: pltpu.VMEM((C_WIDTH // 128, ts * nb, 128), F32)
    const = lambda stack: pl.BlockSpec((None, None) + stack.shape[2:], lambda i: tuple(index) + (0, 0))
    return pl.pallas_call(
        functools.partial(_s5_kernel, nb=nb, reverse=reverse),
        grid=(n_all,),
        in_specs=[pl.BlockSpec((ts, nb * C_WIDTH), tile), const(a), const(wb), const(wc)],
        out_specs=pl.BlockSpec((ts, nb * C_WIDTH), tile),
        out_shape=jax.ShapeDtypeStruct((tt, nb * C_WIDTH), F32),
        scratch_shapes=[split(), split(), pltpu.VMEM((nb, 2 * C_NSTATE), F32)],
        compiler_params=_cparams(1, V7X_VMEM_LIMIT),
        name="s5_scan",
    )(pu, a, wb, wc)


def _s5_params(a_re, a_im, log_dt, b_re, b_im, c_re, c_im):
    eye = jnp.eye(C_GROUPS, dtype=F32)
    dt = jnp.exp(log_dt)[..., None]
    mag = jnp.exp(a_re * dt)
    ang = a_im * dt
    abar_re, abar_im = mag * jnp.cos(ang), mag * jnp.sin(ang)
    den = a_re * a_re + a_im * a_im
    coef_re = ((abar_re - 1.0) * a_re + abar_im * a_im) / den
    coef_im = (abar_im * a_re - (abar_re - 1.0) * a_im) / den
    b_re, b_im = b_re[:, None], b_im[:, None]
    bbar_re = coef_re[..., None] * b_re - coef_im[..., None] * b_im
    bbar_im = coef_re[..., None] * b_im + coef_im[..., None] * b_re
    lead = a_re.shape[:2]
    a = jnp.stack([abar_re.reshape(lead + (C_NSTATE,)), abar_im.reshape(lead + (C_NSTATE,))], axis=2)
    drive = lambda bb: jnp.einsum('lkgpc,gh->lkgchp', bb, eye).reshape(lead + (C_WIDTH, C_NSTATE))
    read = lambda cc: jnp.einsum('lkgcp,gh->lkgphc', cc, eye).reshape(lead + (C_NSTATE, C_WIDTH))
    wb = jnp.concatenate([drive(bbar_re), drive(bbar_im)], axis=-1)
    wc = jnp.concatenate([read(c_re), -read(c_im)], axis=-2)
    return a, wb.astype(BF16), wc.astype(BF16)


def _out_kernel(x_ref, gt_c, gt_b, a_ref, b_ref, u_ref, yf_ref, yb_ref, d_ref, gw_ref, gb_ref, wo_ref, o_ref,
                *, lc):
    y = d_ref[...] * u_ref[...] + yf_ref[...] + yb_ref[...]
    gelu = 0.5 * y * (1.0 + jnp.tanh(math.sqrt(2.0 / math.pi) * (y + 0.044715 * (y * y * y))))
    hg = jnp.dot(gelu.astype(BF16), gw_ref[...], preferred_element_type=F32) + gb_ref[...]
    c = hg[:, :C_WIDTH] * jax.nn.sigmoid(hg[:, C_WIDTH:])
    mix = (jnp.dot(a_ref[...].astype(BF16), wo_ref[0:A_WIDTH, :], preferred_element_type=F32)
           + jnp.dot(b_ref[...].astype(BF16), wo_ref[A_WIDTH:A_WIDTH + B_WIDTH, :], preferred_element_type=F32)
           + jnp.dot(c.astype(BF16), wo_ref[A_WIDTH + B_WIDTH:, :], preferred_element_type=F32))
    x = x_ref[...]
    o_ref[...] = x + jnp.where(_is_ctx_row(x.shape[0], lc), gt_c[0], gt_b[0]) * mix


def _out_proj(X, mod, base, a_o, b_o, pu, y2, s5_d, glu_w_s, glu_b, w_out_s, layer, nb, lc):
    tt = X.shape[0]
    d = D_MODEL
    tm = TOKEN_TILE
    const = lambda shape: pl.BlockSpec(shape, lambda i, b: (0,) * len(shape), pipeline_mode=pl.Buffered(1))
    tile = lambda w: pl.BlockSpec((tm, w), lambda i, b: (i, b))
    return pl.pallas_call(
        functools.partial(_out_kernel, lc=lc),
        grid=(tt // tm, nb),
        in_specs=[tile(d)] + _mod_specs(nb, base, (2,)) + [
                  tile(A_WIDTH), tile(B_WIDTH), tile(C_WIDTH),
                  tile(C_WIDTH), tile(C_WIDTH),
                  const((1, C_WIDTH)), _layer_weight(glu_w_s, (layer,)), const((1, 2 * C_WIDTH)),
                  _layer_weight(w_out_s, (layer,))],
        out_specs=tile(d),
        out_shape=jax.ShapeDtypeStruct(X.shape, F32),
        input_output_aliases={0: 0},
        compiler_params=_cparams(2),
        name="out_proj",
    )(X, mod, mod, a_o, b_o, pu, y2[0], y2[1], s5_d.reshape(1, C_WIDTH), glu_w_s, glu_b.reshape(1, 2 * C_WIDTH), w_out_s)


def _rope_tables(length, lc):
    t = jnp.arange(length)
    pos = jnp.stack([(t // GRID_W).astype(F32), (t % GRID_W).astype(F32)], axis=1)
    inv_freq = ROPE_BASE ** (-jnp.arange(ROPE_PAIRS, dtype=F32) / ROPE_PAIRS)
    ang = pos[:, :, None] * inv_freq
    cos = jnp.cos(ang)[:, :, None, :]
    sin = jnp.sin(ang)[:, :, None, :]
    cos = jnp.broadcast_to(cos, (length, 2, 2, ROPE_PAIRS)).reshape(length, B_HEAD_DIM)
    sin = jnp.concatenate([-sin, sin], axis=2).reshape(length, B_HEAD_DIM)
    cos = jnp.concatenate([jnp.ones((lc, B_HEAD_DIM), F32), cos], axis=0)
    sin = jnp.concatenate([jnp.zeros((lc, B_HEAD_DIM), F32), sin], axis=0)
    return jnp.tile(cos, (1, 128 // B_HEAD_DIM)), jnp.tile(sin, (1, 128 // B_HEAD_DIM))


def kernel(x, c, ctx, c_ctx, ada_w, ada_b, norm_g, ffn_w1, ffn_w2, w_in, w_out, hgrn_lower_bounds,
           hgrn_norm_g, attn_sink, s5_a_re, s5_a_im, s5_log_dt, s5_b_re, s5_b_im, s5_c_re, s5_c_im,
           s5_d, s5_glu_w, s5_glu_b, final_norm_g):
    nb, length, d = x.shape
    lc = ctx.shape[1]
    tt = lc + length
    depth = ada_w.shape[0]
    assert d == D_MODEL and nb == 8 and lc % ROW_TILE == 0 and length % ROW_TILE == 0 and tt % TOKEN_TILE == 0

    act = jnp.concatenate([c, c_ctx[None], jnp.zeros((16 - nb - 1, d), F32)], axis=0)
    mod = _modulation(act, ada_w, ada_b).reshape(depth, 16, 3, 3, d)[:, :nb + 1]
    mod = mod.transpose(0, 2, 1, 3, 4).reshape(depth * 3 * (nb + 1) * 3, 1, d)
    cos_t, sin_t = _rope_tables(length, lc)
    lb_soft = jax.nn.softmax(hgrn_lower_bounds.astype(F32), axis=0)
    lower_bound = jnp.cumsum(lb_soft, axis=0) - lb_soft[0]

    w1s, w2s = ffn_w1.astype(BF16), ffn_w2.astype(BF16)
    w_in_s, w_out_s, glu_w_s = w_in.astype(BF16), w_out.astype(BF16), s5_glu_w.astype(BF16)
    s5a, s5wb, s5wc = _s5_params(s5_a_re, s5_a_im, s5_log_dt, s5_b_re, s5_b_im, s5_c_re, s5_c_im)
    gains = norm_g.reshape(depth * 3, 1, d)
    X = (ctx, x)
    for l in range(depth):
        base = [(l * 3 + sub) * (nb + 1) * 3 for sub in range(3)]
        X = _ffn(X, mod, base[0], gains, l * 3, w1s, w2s, (l, 0), nb, lc)
        pa, pq, pqs, pk, pv, pu = _proj(X, mod, base[1], gains, l * 3 + 1, w_in_s, l, cos_t, sin_t, nb, lc)
        a_o = _hgrn(pa, lower_bound[l], hgrn_norm_g[l], nb, lc)
        b_o = _attn(pq, pqs, pk, pv, attn_sink[l], nb, lc)
        y2 = [_s5_scan(pu, s5a, s5wb, s5wc, (l, k), nb, lc, k == 1) for k in range(2)]
        X = _out_proj(X, mod, base[1], a_o, b_o, pu, y2, s5_d[l], glu_w_s, s5_glu_b[l], w_out_s, l, nb, lc)
        X = _ffn(X, mod, base[2], gains, l * 3 + 2, w1s, w2s, (l, 1), nb, lc,
                 final_g=final_norm_g if l == depth - 1 else None)
    return X
```

```python
import functools
import math

import jax
import jax.numpy as jnp
from jax import lax
from jax.experimental import pallas as pl
from jax.experimental.pallas import tpu as pltpu

F32 = jnp.float32
BF16 = jnp.bfloat16

D_MODEL = 1024
DEPTH = 4
GRID_W = 64
RMS_EPS = 1e-6
N_MOD = 9
D_FF = 2816
A_HEADS = 4
A_HEAD_DIM = 64
A_WIDTH = A_HEADS * A_HEAD_DIM
GLR_CHUNK = 16
B_Q_HEADS = 8
B_KV_HEADS = 2
B_GROUP = B_Q_HEADS // B_KV_HEADS
B_HEAD_DIM = 64
B_WIDTH = B_Q_HEADS * B_HEAD_DIM
B_KV_WIDTH = B_KV_HEADS * B_HEAD_DIM
WINDOW = 128
ATTN_BLOCK = 128
ATTN_Q_BLOCKS = 2
ROPE_BASE = 10000.0
ROPE_PAIRS = B_HEAD_DIM // 4
MASK_VALUE = -1e9
LOG2E = math.log2(math.e)
C_GROUPS = 16
C_GROUP_CH = 16
C_WIDTH = C_GROUPS * C_GROUP_CH
C_STATE = 64
C_NSTATE = C_GROUPS * C_STATE
D_MIX = A_WIDTH + B_WIDTH + C_WIDTH
A_IN = 5 * A_WIDTH
D_IN = A_IN + B_WIDTH + 2 * B_KV_WIDTH + C_WIDTH

V7X_VMEM_LIMIT = 56 * 1024 * 1024
ROW_TILE = 256
TOKEN_TILE = 576
S5_TILE = 128
S5_GROUP = 16
HGRN_CHUNK = 128
HGRN_SUB = 8


def _cparams(n_axes, vmem=None):
    return pltpu.CompilerParams(dimension_semantics=("arbitrary",) * n_axes, vmem_limit_bytes=vmem)


def _split_hi_lo(x):
    hi = x.astype(BF16)
    lo = (x - hi.astype(F32)).astype(BF16)
    return hi, lo


def _mod_kernel(act_ref, w_ref, b_ref, o_ref):
    a = act_ref[...]
    a = (a * jax.nn.sigmoid(a)).astype(BF16)
    o_ref[0] = jnp.dot(a, w_ref[0].astype(BF16), preferred_element_type=F32) + b_ref[0]


def _modulation(act, ada_w, ada_b):
    depth, d, n = ada_w.shape
    tn = 1024
    return pl.pallas_call(
        _mod_kernel,
        grid=(depth, n // tn),
        in_specs=[pl.BlockSpec((16, d), lambda l, j: (0, 0)),
                  pl.BlockSpec((1, d, tn), lambda l, j: (l, 0, j)),
                  pl.BlockSpec((1, 1, tn), lambda l, j: (l, 0, j))],
        out_specs=pl.BlockSpec((1, 16, tn), lambda l, j: (l, 0, j)),
        out_shape=jax.ShapeDtypeStruct((depth, 16, n), F32),
        compiler_params=_cparams(2),
        name="adaln_mod",
    )(act, ada_w, ada_b.reshape(depth, 1, n))


def _is_ctx_row(tm, lc, first_tile=0):
    return lax.broadcasted_iota(jnp.int32, (tm, D_MODEL), 0) < lc - (pl.program_id(0) + first_tile) * tm


def _adaln(x, g, is_ctx, sh_c, sh_b, sc_c, sc_b):
    r = lax.rsqrt(jnp.mean(x * x, axis=-1, keepdims=True) + RMS_EPS)
    gain = jnp.where(is_ctx, g * (1.0 + sc_c[0]), g * (1.0 + sc_b[0]))
    return (x * r) * gain + jnp.where(is_ctx, sh_c[0], sh_b[0])


def _mod_specs(nb, base, comps):
    specs = []
    for comp in comps:
        specs.append(pl.BlockSpec((1, 1, D_MODEL), lambda i, b, comp=comp: (base + nb * 3 + comp, 0, 0)))
        specs.append(pl.BlockSpec((1, 1, D_MODEL), lambda i, b, comp=comp: (base + b * 3 + comp, 0, 0)))
    return specs


def _row_spec(table, row):
    return pl.BlockSpec((1, 1, table.shape[-1]), lambda i, b: (row, 0, 0))


def _ffn_rows(x, is_ctx, sh_c, sh_b, sc_c, sc_b, gt_c, gt_b, g_ref, w1_ref, w2_ref):
    h = _adaln(x, g_ref[0], is_ctx, sh_c, sh_b, sc_c, sc_b).astype(BF16)
    gu = jnp.dot(h, w1_ref[...], preferred_element_type=F32)
    gate = gu[:, :D_FF]
    a = (gate * jax.nn.sigmoid(gate) * gu[:, D_FF:]).astype(BF16)
    y = jnp.dot(a, w2_ref[...], preferred_element_type=F32)
    return x + jnp.where(is_ctx, 0.5 * gt_c[0], 0.5 * gt_b[0]) * y


def _ffn_kernel(x_ref, *refs, lc):
    x = x_ref[...]
    refs[-1][...] = _ffn_rows(x, _is_ctx_row(x.shape[0], lc), *refs[:-1])


def _ffn_entry_kernel(ctx_ref, lat_ref, *refs, lc):
    x = jnp.where(pl.program_id(0) * ctx_ref.shape[1] < lc, ctx_ref[0], lat_ref[0])
    refs[-1][...] = _ffn_rows(x, _is_ctx_row(x.shape[0], lc), *refs[:-1])


def _ffn_exit_kernel(x_ref, *refs, lc):
    x = x_ref[...]
    fg_ref, o_ref = refs[-2:]
    y = _ffn_rows(x, _is_ctx_row(x.shape[0], lc, lc // x.shape[0]), *refs[:-2])
    o_ref[0] = y * lax.rsqrt(jnp.mean(y * y, axis=-1, keepdims=True) + RMS_EPS) * fg_ref[...]


def _layer_weight(stack, index):
    lead = len(index)
    return pl.BlockSpec((None,) * lead + stack.shape[lead:], lambda i, b: tuple(index) + (0, 0),
                        pipeline_mode=pl.Buffered(1))


def _ffn(X, mod, base, norm_g, g_row, w1s, w2s, index, nb, lc, final_g=None):
    d = D_MODEL
    rest = _mod_specs(nb, base, (0, 1, 2)) + [_row_spec(norm_g, g_row),
                                              _layer_weight(w1s, index), _layer_weight(w2s, index)]
    stream_tile = lambda tm: pl.BlockSpec((tm, d), lambda i, b: (i, b))
    extra, alias = (), {}
    if isinstance(X, tuple):
        ctx, lat = X
        tm = ROW_TILE
        nct = lc // tm
        tt = lc + lat.shape[1]
        body = functools.partial(_ffn_entry_kernel, lc=lc)
        srcs = [pl.BlockSpec((1, tm, d), lambda i, b: (b, jnp.minimum(i, nct - 1), 0)),
                pl.BlockSpec((1, tm, d), lambda i, b: (b, jnp.maximum(i - nct, 0), 0))]
        args = (ctx, lat)
        n_tiles, out_spec, out_shape = tt // tm, stream_tile(tm), (tt, nb * d)
    elif final_g is not None:
        tm = ROW_TILE
        nct = lc // tm
        tt = X.shape[0]
        body = functools.partial(_ffn_exit_kernel, lc=lc)
        srcs, args = [pl.BlockSpec((tm, d), lambda i, b: (i + nct, b))], (X,)
        rest = rest + [pl.BlockSpec((1, d), lambda i, b: (0, 0))]
        extra = (final_g.reshape(1, d),)
        n_tiles, out_spec, out_shape = (tt - lc) // tm, pl.BlockSpec((1, tm, d), lambda i, b: (b, i, 0)), (nb, tt - lc, d)
    else:
        tm = TOKEN_TILE
        tt = X.shape[0]
        body = functools.partial(_ffn_kernel, lc=lc)
        srcs, args, alias = [stream_tile(tm)], (X,), {0: 0}
        n_tiles, out_spec, out_shape = tt // tm, stream_tile(tm), (tt, nb * d)
    return pl.pallas_call(
        body,
        grid=(n_tiles, nb),
        in_specs=srcs + rest,
        out_specs=out_spec,
        out_shape=jax.ShapeDtypeStruct(out_shape, F32),
        input_output_aliases=alias,
        compiler_params=_cparams(2, V7X_VMEM_LIMIT),
        name="ffn_half_step",
    )(*args, *([mod] * 6), norm_g, w1s, w2s, *extra)


def _rope(x, cos, sin, first_half):
    w = x.shape[-1]
    nxt = pltpu.roll(x, w - ROPE_PAIRS, axis=1)
    prv = pltpu.roll(x, ROPE_PAIRS, axis=1)
    return x * cos + jnp.where(first_half, nxt, prv) * sin


def _proj_kernel(x_ref, sh_c, sh_b, sc_c, sc_b, g_ref, w_ref, cos_ref, sin_ref,
                 pa_ref, pq_ref, pqs_ref, pk_ref, pv_ref, pu_ref, *, lc):
    x = x_ref[...]
    h = _adaln(x, g_ref[0], _is_ctx_row(x.shape[0], lc), sh_c, sh_b, sc_c, sc_b).astype(BF16)
    p = jnp.dot(h, w_ref[...], preferred_element_type=F32)
    o = A_IN
    pa_ref[...] = p[:, :o]
    cos = cos_ref[...]
    sin = sin_ref[...]
    lane = lax.broadcasted_iota(jnp.int32, (1, B_WIDTH), 1)
    first_half = (lane % (2 * ROPE_PAIRS)) < ROPE_PAIRS
    cos_q = jnp.concatenate([cos] * (B_WIDTH // 128), axis=1)
    sin_q = jnp.concatenate([sin] * (B_WIDTH // 128), axis=1)
    q = _rope(p[:, o:o + B_WIDTH], cos_q, sin_q, first_half)
    q = q * (B_HEAD_DIM ** -0.5 * LOG2E)
    pq_ref[...] = q.astype(BF16)
    low_head = (lane % 128) < B_HEAD_DIM
    q_sw = jnp.where(low_head, pltpu.roll(q, B_WIDTH - B_HEAD_DIM, axis=1), pltpu.roll(q, B_HEAD_DIM, axis=1))
    pqs_ref[...] = q_sw.astype(BF16)
    o += B_WIDTH
    k = _rope(p[:, o:o + B_KV_WIDTH], cos, sin, first_half[:, :B_KV_WIDTH])
    pk_ref[...] = k.astype(BF16)
    o += B_KV_WIDTH
    pv_ref[...] = p[:, o:o + B_KV_WIDTH].astype(BF16)
    o += B_KV_WIDTH
    pu_ref[...] = p[:, o:o + C_WIDTH]


def _proj(X, mod, base, norm_g, g_row, w_in_s, layer, cos_t, sin_t, nb, lc):
    tt = X.shape[0]
    d = D_MODEL
    tm = TOKEN_TILE
    const = lambda shape: pl.BlockSpec(shape, lambda i, b: (0,) * len(shape), pipeline_mode=pl.Buffered(1))
    tile = lambda w: pl.BlockSpec((tm, w), lambda i, b: (i, b))
    widths = (A_IN, B_WIDTH, B_WIDTH, B_KV_WIDTH, B_KV_WIDTH, C_WIDTH)
    dtypes = (F32, BF16, BF16, BF16, BF16, F32)
    return pl.pallas_call(
        functools.partial(_proj_kernel, lc=lc),
        grid=(tt // tm, nb),
        in_specs=[tile(d)] + _mod_specs(nb, base, (0, 1)) + [
                  _row_spec(norm_g, g_row), _layer_weight(w_in_s, (layer,)),
                  pl.BlockSpec((tm, 128), lambda i, b: (i, 0)), pl.BlockSpec((tm, 128), lambda i, b: (i, 0))],
        out_specs=[tile(w) for w in widths],
        out_shape=[jax.ShapeDtypeStruct((tt, nb * w), dt) for w, dt in zip(widths, dtypes)],
        compiler_params=_cparams(2, V7X_VMEM_LIMIT),
        name="in_proj",
    )(X, mod, mod, mod, mod, norm_g, w_in_s, cos_t, sin_t)


def _hgrn_pairs(lo, hi, reverse):
    if hi - lo <= HGRN_SUB:
        return []
    mid = (lo + hi) // 2
    here = (lo, mid, mid - lo, mid) if reverse else (mid, lo, mid - lo, mid - 1)
    return [here] + _hgrn_pairs(lo, mid, reverse) + _hgrn_pairs(mid, hi, reverse)


def _hgrn_kernel(pa_ref, lb_ref, ng_ref, o_ref,
                 qs_s, kf_s, kb_s, ks_s, bf_s, bb_s, v_s, o_s, p_s, st_s, mask_s, ones_s, tri_s, *, lc):
    tt = pa_ref.shape[0]
    w = A_WIDTH
    ch = HGRN_CHUNK
    sub = HGRN_SUB
    n_sub = ch // sub
    blk = 128
    n_blk = tt // blk
    n_chunks = tt // ch
    n_ctx_chunks = lc // ch
    nt = (((1,), (1,)), ((), ()))
    tn = (((0,), (0,)), ((), ()))

    row = lax.broadcasted_iota(jnp.int32, (w, w), 0)
    col = lax.broadcasted_iota(jnp.int32, (w, w), 1)
    same_head = jnp.where((row // A_HEAD_DIM) == (col // A_HEAD_DIM), 1.0, 0.0)
    mask_s[...] = same_head
    ones_s[...] = same_head.astype(BF16)
    lane = lax.broadcasted_iota(jnp.int32, (1, w), 1)
    head_mask = [jnp.where(lane // A_HEAD_DIM == h, 1.0, 0.0) for h in range(A_HEADS)]
    r = lax.broadcasted_iota(jnp.int32, (blk, blk), 0)
    c = lax.broadcasted_iota(jnp.int32, (blk, blk), 1)
    same_chunk = (r // ch) == (c // ch)
    tri_s[0] = jnp.where(same_chunk, jnp.where(c <= r, 1.0, 0.0), 0.0).astype(BF16)
    tri_s[1] = jnp.where(same_chunk, jnp.where(c >= r, 1.0, 0.0), 0.0).astype(BF16)
    st_s[...] = jnp.zeros(st_s.shape, F32)

    def chunk_cumsum(logf, tri):
        hi = logf.astype(BF16)
        rem = logf - hi.astype(F32)
        mid = rem.astype(BF16)
        lo = (rem - mid.astype(F32)).astype(BF16)
        return (jnp.dot(tri, hi, preferred_element_type=F32)
                + jnp.dot(tri, mid, preferred_element_type=F32)
                + jnp.dot(tri, lo, preferred_element_type=F32))

    def load(ref, rows):
        return jnp.concatenate([ref[c, rows, :] for c in range(w // 128)], axis=1)

    def store(ref, rows, val):
        for c in range(w // 128):
            ref[c, rows, :] = val[:, c * 128:(c + 1) * 128]

    def load_row(ref, r):
        return load(ref, pl.ds(r, sub, stride=0))

    def prep_body(i, carry):
        rows = pl.ds(pl.multiple_of(i * blk, blk), blk)
        q = pa_ref[rows, 0:w]
        qs_s[rows, :] = q * jax.nn.sigmoid(q)
        store(v_s, rows, pa_ref[rows, w:2 * w])
        o_s[rows, :] = jnp.zeros((blk, w), F32)
        k_sum = None
        for d, (k_s, b_s) in enumerate(((kf_s, bf_s), (kb_s, bb_s))):
            lb = lb_ref[d:d + 1, :]
            z = pa_ref[rows, (2 + d) * w:(3 + d) * w]
            t = jnp.exp(-jnp.abs(z))
            big = 1.0 / (1.0 + t)
            small = t * big
            f = lb + (1.0 - lb) * jnp.where(z >= 0, big, small)
            k = (1.0 - lb) * jnp.where(z >= 0, small, big)
            store(k_s, rows, k)
            k_sum = k if k_sum is None else k_sum + k
            store(b_s, rows, chunk_cumsum(jnp.log2(f), tri_s[d]))
        store(ks_s, rows, k_sum)
        return carry
    lax.fori_loop(0, n_blk, prep_body, 0)

    def expand(blocks):
        return jnp.concatenate([x * head_mask[h] for h in range(A_HEADS) for x in blocks], axis=0).astype(BF16)

    def expand_cached(masked, s0, n):
        j0 = s0 // sub
        return jnp.concatenate([masked[j][h] for h in range(A_HEADS) for j in range(j0, j0 + n // sub)],
                               axis=0).astype(BF16)

    def diag_scores(r0):
        t_i = lax.broadcasted_iota(jnp.int32, (sub, w), 0)
        for j in range(n_sub):
            base = r0 + j * sub
            rows = pl.ds(base, sub)
            q = qs_s[rows, :]
            bf = load(bf_s, rows)
            bb = load(bb_s, rows)
            for s in range(sub):
                kf_row = load_row(kf_s, base + s)
                kb_row = load_row(kb_s, base + s)
                arg = jnp.where(t_i >= s, bf - load_row(bf_s, base + s), bb - load_row(bb_s, base + s))
                kk = jnp.where(t_i > s, kf_row, jnp.where(t_i < s, kb_row, load_row(ks_s, base + s)))
                p_s[(j * sub + s) * sub:(j * sub + s + 1) * sub, :] = q * jnp.exp2(arg) * kk
        return jnp.dot(p_s[...].astype(BF16), ones_s[...], preferred_element_type=F32)

    def diag_apply(r0, a):
        out = []
        for j in range(n_sub):
            acc = None
            for s in range(sub):
                v_row = load_row(v_s, r0 + j * sub + s)
                term = a[(j * sub + s) * sub:(j * sub + s + 1) * sub, :] * v_row
                acc = term if acc is None else acc + term
            out.append(acc)
        return out

    def dir_scores(r0, d):
        k_s, b_s = (kf_s, bf_s) if d == 0 else (kb_s, bb_s)
        rows8 = lambda off: pl.ds(r0 + off, sub)
        q_blk = lambda off: qs_s[rows8(off), :]
        b_blk = lambda off: load(b_s, rows8(off))
        k_blk = lambda off: load(k_s, rows8(off))
        offs = lambda start, n: range(start, start + n, sub)
        b_last = load_row(b_s, r0 + ((ch - 1) if d == 0 else 0))
        pairs = _hgrn_pairs(0, ch, d == 1)
        scores = []
        for t0, s0, n, ref in pairs:
            b_ref = load_row(b_s, r0 + ref)
            qx = jnp.concatenate([q_blk(o) * jnp.exp2(b_blk(o) - b_ref) for o in offs(t0, n)], axis=0).astype(BF16)
            kx = expand([k_blk(o) * jnp.exp2(b_ref - b_blk(o)) for o in offs(s0, n)])
            scores.append(lax.dot_general(qx, kx, nt, preferred_element_type=F32))
        st = st_s[d]
        qe = jnp.concatenate([q_blk(o) * jnp.exp2(b_blk(o)) for o in offs(0, ch)], axis=0).astype(BF16)
        kd = jnp.concatenate([k_blk(o) * jnp.exp2(b_last - b_blk(o)) for o in offs(0, ch)], axis=0).astype(BF16)
        o = lax.dot_general(qe, st.astype(BF16), nt, preferred_element_type=F32)
        ut = lax.dot_general(load(v_s, pl.ds(r0, ch)).astype(BF16), kd, tn, preferred_element_type=F32)
        return pairs, scores, o, ut, st, b_last[0:1, :]

    def masked_values(r0):
        blocks = [load(v_s, pl.ds(r0 + j * sub, sub)) for j in range(n_sub)]
        return [[v * head_mask[h] for h in range(A_HEADS)] for v in blocks]

    def dir_values(pairs, scores, v_masked):
        return [jnp.dot(sc.astype(BF16), expand_cached(v_masked, s0, n), preferred_element_type=F32)
                for (t0, s0, n, ref), sc in zip(pairs, scores)]

    def dir_finish(d, pairs, contribs, o, ut, st, b_last):
        st_s[d] = st * jnp.exp2(b_last) + ut * mask_s[...]
        out = [o[j * sub:(j + 1) * sub, :] for j in range(n_sub)]
        for (t0, s0, n, ref), contrib in zip(pairs, contribs):
            for jj in range(n // sub):
                out[t0 // sub + jj] = out[t0 // sub + jj] + contrib[jj * sub:(jj + 1) * sub, :]
        return out

    def chunk_body(i, carry):
        cb = jnp.where(i < n_ctx_chunks, n_ctx_chunks - 1 - i, n_chunks + n_ctx_chunks - 1 - i)
        rf = pl.multiple_of(i * ch, ch)
        rb = pl.multiple_of(cb * ch, ch)
        pf, scf, of, utf, stf, blf = dir_scores(rf, 0)
        pb, scb, ob, utb, stb, blb = dir_scores(rb, 1)
        a = diag_scores(rf)
        cf = dir_values(pf, scf, masked_values(rf))
        cbw = dir_values(pb, scb, masked_values(rb))
        o_fwd = dir_finish(0, pf, cf, of, utf, stf, blf)
        o_bwd = dir_finish(1, pb, cbw, ob, utb, stb, blb)
        o_diag = diag_apply(rf, a)
        rows_f = pl.ds(rf, ch)
        o_s[rows_f, :] = o_s[rows_f, :] + jnp.concatenate([x + y for x, y in zip(o_diag, o_fwd)], axis=0)
        rows_b = pl.ds(rb, ch)
        o_s[rows_b, :] = o_s[rows_b, :] + jnp.concatenate(o_bwd, axis=0)
        return carry
    lax.fori_loop(0, n_chunks, chunk_body, 0)

    def out_body(i, carry):
        rows = pl.ds(pl.multiple_of(i * blk, blk), blk)
        o = o_s[rows, :]
        hi, lo = _split_hi_lo(o * o)
        ones = ones_s[...]
        ms = (jnp.dot(hi, ones, preferred_element_type=F32)
              + jnp.dot(lo, ones, preferred_element_type=F32)) * (1.0 / A_HEAD_DIM)
        g = pa_ref[rows, 4 * w:5 * w]
        o_ref[rows, :] = o * lax.rsqrt(ms + RMS_EPS) * ng_ref[...] * (g * jax.nn.sigmoid(g))
        return carry
    lax.fori_loop(0, n_blk, out_body, 0)


def _hgrn(pa, lower_bound, norm_g, nb, lc):
    tt = pa.shape[0]
    w = A_WIDTH
    seq = lambda: pltpu.VMEM((tt, w), F32)
    split = lambda: pltpu.VMEM((w // 128, tt, 128), F32)
    return pl.pallas_call(
        functools.partial(_hgrn_kernel, lc=lc),
        grid=(nb,),
        in_specs=[pl.BlockSpec((tt, A_IN), lambda b: (0, b)),
                  pl.BlockSpec((2, w), lambda b: (0, 0)),
                  pl.BlockSpec((1, w), lambda b: (0, 0))],
        out_specs=pl.BlockSpec((tt, w), lambda b: (0, b)),
        out_shape=jax.ShapeDtypeStruct((tt, nb * w), F32),
        scratch_shapes=[seq(), split(), split(), split(), split(), split(), split(), seq(),
                        pltpu.VMEM((HGRN_CHUNK * HGRN_SUB, w), F32),
                        pltpu.VMEM((2, w, w), F32), pltpu.VMEM((w, w), F32), pltpu.VMEM((w, w), BF16),
                        pltpu.VMEM((2, 128, 128), BF16)],
        compiler_params=_cparams(1, V7X_VMEM_LIMIT),
        name="hgrn2_mixer",
    )(pa, lower_bound, norm_g.reshape(1, w))


def _attn_kernel(sink_ref, q_ref, qs_ref, k_ref, v_ref, o_ref, kz_s, *, lc):
    tt = k_ref.shape[0]
    blk = ATTN_BLOCK
    hd = B_HEAD_DIM
    j = pl.program_id(1)
    n_ctx_blk = lc // blk
    n_blk = tt // blk
    n_sub = q_ref.shape[0] // blk
    gw = B_GROUP * blk
    nt = (((1,), (1,)), ((), ()))
    tn = (((0,), (0,)), ((), ()))

    @pl.when(j == 0)
    def _():
        lane = lax.broadcasted_iota(jnp.int32, (1, B_KV_WIDTH), 1)
        for g in range(B_KV_HEADS):
            keep = (lane // hd) == g

            def body(i, carry):
                rows = pl.ds(pl.multiple_of(i * blk, blk), blk)
                k = k_ref[rows, :]
                kz_s[g, rows, :] = jnp.where(keep, k, jnp.zeros_like(k))
                return carry
            lax.fori_loop(0, n_blk, body, 0)

    def q_stack(g, qi):
        parts = []
        for hh in range(B_GROUP):
            h = g * B_GROUP + hh
            src = q_ref if (h % 2) == g else qs_ref
            parts.append(src[qi * blk:(qi + 1) * blk, (h // 2) * 128:(h // 2 + 1) * 128])
        return jnp.concatenate(parts, axis=0)

    def scores(g, qi, rows):
        return lax.dot_general(kz_s[g, rows, :], q_stack(g, qi), nt, preferred_element_type=F32)

    def softmax_values(g, pieces):
        sink = jnp.concatenate([jnp.full((1, blk), sink_ref[g * B_GROUP + hh] * LOG2E, F32)
                                for hh in range(B_GROUP)], axis=1)
        m = sink
        for s, _ in pieces:
            m = jnp.maximum(m, jnp.max(s, axis=0, keepdims=True))
        den = jnp.exp2(sink - m)
        probs = []
        for s, _ in pieces:
            p = jnp.exp2(s - m)
            den = den + jnp.sum(p, axis=0, keepdims=True)
            probs.append(p.astype(BF16))
        values = jnp.concatenate([vv for _, vv in pieces], axis=0)
        acc = lax.dot_general(values, jnp.concatenate(probs, axis=0), tn,
                              preferred_element_type=F32)
        return acc[g * hd:(g + 1) * hd, :] * (1.0 / den)

    def write(qi, outs):
        for pair in range(B_Q_HEADS // 2):
            g, hh = (2 * pair) // B_GROUP, (2 * pair) % B_GROUP
            both = jnp.concatenate([outs[g][:, hh * blk:(hh + 1) * blk],
                                    outs[g][:, (hh + 1) * blk:(hh + 2) * blk]], axis=0)
            o_ref[qi * blk:(qi + 1) * blk, pair * 128:(pair + 1) * 128] = both.T

    def ctx_scores(qi):
        return [[scores(g, qi, pl.ds(0, lc))] for g in range(B_KV_HEADS)]

    def ctx_pieces(qi, sc):
        return [[(sc[g][0], v_ref[0:lc, :])] for g in range(B_KV_HEADS)]

    def band_rows(jb):
        prev = pl.ds(pl.multiple_of((jb - 1) * blk, blk), blk)
        own = pl.ds(pl.multiple_of(jb * blk, blk), blk)
        nxt = pl.ds(pl.multiple_of(jnp.minimum(jb + 1, n_blk - 1) * blk, blk), blk)
        return prev, own, nxt, pl.ds(0, lc)

    def band_scores(qi, jb):
        return [[scores(g, qi, r) for r in band_rows(jb)] for g in range(B_KV_HEADS)]

    def band_pieces(jb, sc):
        key = lax.broadcasted_iota(jnp.int32, (blk, gw), 0)
        qry = lax.broadcasted_iota(jnp.int32, (blk, gw), 1) % blk
        keep_prev = key >= qry + jnp.where(jb > n_ctx_blk, 0, blk)
        keep_next = key <= qry - jnp.where(jb < n_blk - 1, 0, blk)
        masked = MASK_VALUE * LOG2E
        r_prev, r_own, r_next, r_ctx = band_rows(jb)
        out = []
        for g in range(B_KV_HEADS):
            sp, so, sn, sx = sc[g]
            out.append([(jnp.where(keep_prev, sp, masked), v_ref[r_prev, :]), (so, v_ref[r_own, :]),
                        (jnp.where(keep_next, sn, masked), v_ref[r_next, :]), (sx, v_ref[r_ctx, :])])
        return out

    first = j * n_sub

    @pl.when(first < n_ctx_blk)
    def _():
        sc = [ctx_scores(qi) for qi in range(n_sub)]
        for qi in range(n_sub):
            pieces = ctx_pieces(qi, sc[qi])
            write(qi, [softmax_values(g, pieces[g]) for g in range(B_KV_HEADS)])

    @pl.when(first >= n_ctx_blk)
    def _():
        sc = [band_scores(qi, first + qi) for qi in range(n_sub)]
        for qi in range(n_sub):
            pieces = band_pieces(first + qi, sc[qi])
            write(qi, [softmax_values(g, pieces[g]) for g in range(B_KV_HEADS)])


def _attn(pq, pqs, pk, pv, sink, nb, lc):
    tt = pq.shape[0]
    blk = ATTN_Q_BLOCKS * ATTN_BLOCK
    assert lc % blk == 0 and tt % blk == 0
    return pl.pallas_call(
        functools.partial(_attn_kernel, lc=lc),
        grid=(nb, tt // blk),
        in_specs=[pl.BlockSpec(memory_space=pltpu.SMEM),
                  pl.BlockSpec((blk, B_WIDTH), lambda b, j: (j, b)),
                  pl.BlockSpec((blk, B_WIDTH), lambda b, j: (j, b)),
                  pl.BlockSpec((tt, B_KV_WIDTH), lambda b, j: (0, b)),
                  pl.BlockSpec((tt, B_KV_WIDTH), lambda b, j: (0, b))],
        out_specs=pl.BlockSpec((blk, B_WIDTH), lambda b, j: (j, b)),
        out_shape=jax.ShapeDtypeStruct((tt, nb * B_WIDTH), F32),
        scratch_shapes=[pltpu.VMEM((B_KV_HEADS, tt, B_KV_WIDTH), BF16)],
        compiler_params=_cparams(2),
        name="window_gqa",
    )(sink, pq, pqs, pk, pv)


def _s5_tile_index(reverse, i, n_ctx, n_all):
    if not reverse:
        return i
    return jnp.where(i < n_ctx, n_ctx - 1 - i, n_all + n_ctx - 1 - i)


def _s5_kernel(u_ref, a_ref, wb_ref, wc_ref, y_ref, u_s, y_s, h_s, *, nb, reverse):
    i = pl.program_id(0)
    n = C_NSTATE
    steps = u_ref.shape[0]
    halves = C_WIDTH // 128
    group = S5_GROUP
    n_groups = steps // group

    @pl.when(i == 0)
    def _():
        h_s[...] = jnp.zeros(h_s.shape, F32)

    for b in range(nb):
        for c in range(halves):
            lanes = slice(b * C_WIDTH + c * 128, b * C_WIDTH + (c + 1) * 128)
            u_s[c, pl.ds(b, steps, stride=nb), :] = u_ref[:, lanes]
    a_re = jnp.broadcast_to(a_ref[0:1, :], (nb, n))
    a_im = jnp.broadcast_to(a_ref[1:2, :], (nb, n))

    def drive(g):
        rows = slice(g * group * nb, (g + 1) * group * nb)
        u = jnp.concatenate([u_s[c, rows, :] for c in range(halves)], axis=1)
        return jnp.dot(u.astype(BF16), wb_ref[...], preferred_element_type=F32)

    order = list(range(n_groups))[::-1] if reverse else list(range(n_groups))
    lookahead = 2
    driven = {g: drive(g) for g in order[:lookahead]}
    h_re, h_im = h_s[:, 0:n], h_s[:, n:2 * n]
    for pos, g in enumerate(order):
        bu = driven.pop(g)
        states = [None] * group
        for t in (range(group - 1, -1, -1) if reverse else range(group)):
            slab = bu[t * nb:(t + 1) * nb, :]
            h_re, h_im = (a_re * h_re - a_im * h_im + slab[:, 0:n],
                          a_re * h_im + a_im * h_re + slab[:, n:2 * n])
            states[t] = jnp.concatenate([h_re, h_im], axis=1)
        if pos + lookahead < n_groups:
            nxt = order[pos + lookahead]
            driven[nxt] = drive(nxt)
        hb = jnp.concatenate(states, axis=0).astype(BF16)
        y = jnp.dot(hb, wc_ref[...], preferred_element_type=F32)
        rows = slice(g * group * nb, (g + 1) * group * nb)
        for c in range(halves):
            y_s[c, rows, :] = y[:, c * 128:(c + 1) * 128]
    h_s[:, 0:n] = h_re
    h_s[:, n:2 * n] = h_im
    for b in range(nb):
        for c in range(halves):
            lanes = slice(b * C_WIDTH + c * 128, b * C_WIDTH + (c + 1) * 128)
            y_ref[:, lanes] = y_s[c, pl.ds(b, steps, stride=nb), :]


def _s5_scan(pu, a, wb, wc, index, nb, lc, reverse):
    tt = pu.shape[0]
    ts = S5_TILE
    n_all = tt // ts
    n_ctx = lc // ts
    tile = lambda i: (_s5_tile_index(reverse, i, n_ctx, n_all), 0)
    split = lambda: pltpu.VMEM((C_WIDTH // 128, ts * nb, 128), F32)
    const = lambda stack: pl.BlockSpec((None, None) + stack.shape[2:], lambda i: tuple(index) + (0, 0))
    return pl.pallas_call(
        functools.partial(_s5_kernel, nb=nb, reverse=reverse),
        grid=(n_all,),
        in_specs=[pl.BlockSpec((ts, nb * C_WIDTH), tile), const(a), const(wb), const(wc)],
        out_specs=pl.BlockSpec((ts, nb * C_WIDTH), tile),
        out_shape=jax.ShapeDtypeStruct((tt, nb * C_WIDTH), F32),
        scratch_shapes=[split(), split(), pltpu.VMEM((nb, 2 * C_NSTATE), F32)],
        compiler_params=_cparams(1, V7X_VMEM_LIMIT),
        name="s5_scan",
    )(pu, a, wb, wc)


def _s5_params(a_re, a_im, log_dt, b_re, b_im, c_re, c_im):
    eye = jnp.eye(C_GROUPS, dtype=F32)
    dt = jnp.exp(log_dt)[..., None]
    mag = jnp.exp(a_re * dt)
    ang = a_im * dt
    abar_re, abar_im = mag * jnp.cos(ang), mag * jnp.sin(ang)
    den = a_re * a_re + a_im * a_im
    coef_re = ((abar_re - 1.0) * a_re + abar_im * a_im) / den
    coef_im = (abar_im * a_re - (abar_re - 1.0) * a_im) / den
    b_re, b_im = b_re[:, None], b_im[:, None]
    bbar_re = coef_re[..., None] * b_re - coef_im[..., None] * b_im
    bbar_im = coef_re[..., None] * b_im + coef_im[..., None] * b_re
    lead = a_re.shape[:2]
    a = jnp.stack([abar_re.reshape(lead + (C_NSTATE,)), abar_im.reshape(lead + (C_NSTATE,))], axis=2)
    drive = lambda bb: jnp.einsum('lkgpc,gh->lkgchp', bb, eye).reshape(lead + (C_WIDTH, C_NSTATE))
    read = lambda cc: jnp.einsum('lkgcp,gh->lkgphc', cc, eye).reshape(lead + (C_NSTATE, C_WIDTH))
    wb = jnp.concatenate([drive(bbar_re), drive(bbar_im)], axis=-1)
    wc = jnp.concatenate([read(c_re), -read(c_im)], axis=-2)
    return a, wb.astype(BF16), wc.astype(BF16)


def _out_kernel(x_ref, gt_c, gt_b, a_ref, b_ref, u_ref, yf_ref, yb_ref, d_ref, gw_ref, gb_ref, wo_ref, o_ref,
                *, lc):
    y = d_ref[...] * u_ref[...] + yf_ref[...] + yb_ref[...]
    gelu = 0.5 * y * (1.0 + jnp.tanh(math.sqrt(2.0 / math.pi) * (y + 0.044715 * (y * y * y))))
    hg = jnp.dot(gelu.astype(BF16), gw_ref[...], preferred_element_type=F32) + gb_ref[...]
    c = hg[:, :C_WIDTH] * jax.nn.sigmoid(hg[:, C_WIDTH:])
    mix = (jnp.dot(a_ref[...].astype(BF16), wo_ref[0:A_WIDTH, :], preferred_element_type=F32)
           + jnp.dot(b_ref[...].astype(BF16), wo_ref[A_WIDTH:A_WIDTH + B_WIDTH, :], preferred_element_type=F32)
           + jnp.dot(c.astype(BF16), wo_ref[A_WIDTH + B_WIDTH:, :], preferred_element_type=F32))
    x = x_ref[...]
    o_ref[...] = x + jnp.where(_is_ctx_row(x.shape[0], lc), gt_c[0], gt_b[0]) * mix


def _out_proj(X, mod, base, a_o, b_o, pu, y2, s5_d, glu_w_s, glu_b, w_out_s, layer, nb, lc):
    tt = X.shape[0]
    d = D_MODEL
    tm = TOKEN_TILE
    const = lambda shape: pl.BlockSpec(shape, lambda i, b: (0,) * len(shape), pipeline_mode=pl.Buffered(1))
    tile = lambda w: pl.BlockSpec((tm, w), lambda i, b: (i, b))
    return pl.pallas_call(
        functools.partial(_out_kernel, lc=lc),
        grid=(tt // tm, nb),
        in_specs=[tile(d)] + _mod_specs(nb, base, (2,)) + [
                  tile(A_WIDTH), tile(B_WIDTH), tile(C_WIDTH),
                  tile(C_WIDTH), tile(C_WIDTH),
                  const((1, C_WIDTH)), _layer_weight(glu_w_s, (layer,)), const((1, 2 * C_WIDTH)),
                  _layer_weight(w_out_s, (layer,))],
        out_specs=tile(d),
        out_shape=jax.ShapeDtypeStruct(X.shape, F32),
        input_output_aliases={0: 0},
        compiler_params=_cparams(2),
        name="out_proj",
    )(X, mod, mod, a_o, b_o, pu, y2[0], y2[1], s5_d.reshape(1, C_WIDTH), glu_w_s, glu_b.reshape(1, 2 * C_WIDTH), w_out_s)


def _rope_tables(length, lc):
    t = jnp.arange(length)
    pos = jnp.stack([(t // GRID_W).astype(F32), (t % GRID_W).astype(F32)], axis=1)
    inv_freq = ROPE_BASE ** (-jnp.arange(ROPE_PAIRS, dtype=F32) / ROPE_PAIRS)
    ang = pos[:, :, None] * inv_freq
    cos = jnp.cos(ang)[:, :, None, :]
    sin = jnp.sin(ang)[:, :, None, :]
    cos = jnp.broadcast_to(cos, (length, 2, 2, ROPE_PAIRS)).reshape(length, B_HEAD_DIM)
    sin = jnp.concatenate([-sin, sin], axis=2).reshape(length, B_HEAD_DIM)
    cos = jnp.concatenate([jnp.ones((lc, B_HEAD_DIM), F32), cos], axis=0)
    sin = jnp.concatenate([jnp.zeros((lc, B_HEAD_DIM), F32), sin], axis=0)
    return jnp.tile(cos, (1, 128 // B_HEAD_DIM)), jnp.tile(sin, (1, 128 // B_HEAD_DIM))


def kernel(x, c, ctx, c_ctx, ada_w, ada_b, norm_g, ffn_w1, ffn_w2, w_in, w_out, hgrn_lower_bounds,
           hgrn_norm_g, attn_sink, s5_a_re, s5_a_im, s5_log_dt, s5_b_re, s5_b_im, s5_c_re, s5_c_im,
           s5_d, s5_glu_w, s5_glu_b, final_norm_g):
    nb, length, d = x.shape
    lc = ctx.shape[1]
    tt = lc + length
    depth = ada_w.shape[0]
    assert d == D_MODEL and nb == 8 and lc % ROW_TILE == 0 and length % ROW_TILE == 0 and tt % TOKEN_TILE == 0

    act = jnp.concatenate([c, c_ctx[None], jnp.zeros((16 - nb - 1, d), F32)], axis=0)
    mod = _modulation(act, ada_w, ada_b).reshape(depth, 16, 3, 3, d)[:, :nb + 1]
    mod = mod.transpose(0, 2, 1, 3, 4).reshape(depth * 3 * (nb + 1) * 3, 1, d)
    cos_t, sin_t = _rope_tables(length, lc)
    lb_soft = jax.nn.softmax(hgrn_lower_bounds.astype(F32), axis=0)
    lower_bound = jnp.cumsum(lb_soft, axis=0) - lb_soft[0]

    w1s, w2s = ffn_w1.astype(BF16), ffn_w2.astype(BF16)
    w_in_s, w_out_s, glu_w_s = w_in.astype(BF16), w_out.astype(BF16), s5_glu_w.astype(BF16)
    s5a, s5wb, s5wc = _s5_params(s5_a_re, s5_a_im, s5_log_dt, s5_b_re, s5_b_im, s5_c_re, s5_c_im)
    gains = norm_g.reshape(depth * 3, 1, d)
    X = (ctx, x)
    for l in range(depth):
        base = [(l * 3 + sub) * (nb + 1) * 3 for sub in range(3)]
        X = _ffn(X, mod, base[0], gains, l * 3, w1s, w2s, (l, 0), nb, lc)
        pa, pq, pqs, pk, pv, pu = _proj(X, mod, base[1], gains, l * 3 + 1, w_in_s, l, cos_t, sin_t, nb, lc)
        a_o = _hgrn(pa, lower_bound[l], hgrn_norm_g[l], nb, lc)
        b_o = _attn(pq, pqs, pk, pv, attn_sink[l], nb, lc)
        y2 = [_s5_scan(pu, s5a, s5wb, s5wc, (l, k), nb, lc, k == 1) for k in range(2)]
        X = _out_proj(X, mod, base[1], a_o, b_o, pu, y2, s5_d[l], glu_w_s, s5_glu_b[l], w_out_s, l, nb, lc)
        X = _ffn(X, mod, base[2], gains, l * 3 + 2, w1s, w2s, (l, 1), nb, lc,
                 final_g=final_norm_g if l == depth - 1 else None)
    return X
```

```python
import functools
import math

import jax
import jax.numpy as jnp
from jax import lax
from jax.experimental import pallas as pl
from jax.experimental.pallas import tpu as pltpu

F32 = jnp.float32
BF16 = jnp.bfloat16

D_MODEL = 1024
DEPTH = 4
GRID_W = 64
RMS_EPS = 1e-6
N_MOD = 9
D_FF = 2816
A_HEADS = 4
A_HEAD_DIM = 64
A_WIDTH = A_HEADS * A_HEAD_DIM
GLR_CHUNK = 16
B_Q_HEADS = 8
B_KV_HEADS = 2
B_GROUP = B_Q_HEADS // B_KV_HEADS
B_HEAD_DIM = 64
B_WIDTH = B_Q_HEADS * B_HEAD_DIM
B_KV_WIDTH = B_KV_HEADS * B_HEAD_DIM
WINDOW = 128
ATTN_BLOCK = 128
ATTN_Q_BLOCKS = 2
ROPE_BASE = 10000.0
ROPE_PAIRS = B_HEAD_DIM // 4
MASK_VALUE = -1e9
LOG2E = math.log2(math.e)
C_GROUPS = 16
C_GROUP_CH = 16
C_WIDTH = C_GROUPS * C_GROUP_CH
C_STATE = 64
C_NSTATE = C_GROUPS * C_STATE
D_MIX = A_WIDTH + B_WIDTH + C_WIDTH
A_IN = 5 * A_WIDTH
D_IN = A_IN + B_WIDTH + 2 * B_KV_WIDTH + C_WIDTH

V7X_VMEM_LIMIT = 56 * 1024 * 1024
ROW_TILE = 256
TOKEN_TILE = 384
S5_TILE = 128
S5_GROUP = 16
HGRN_CHUNK = 128
HGRN_SUB = 8


def _cparams(n_axes, vmem=None):
    return pltpu.CompilerParams(dimension_semantics=("arbitrary",) * n_axes, vmem_limit_bytes=vmem)


def _split_hi_lo(x):
    hi = x.astype(BF16)
    lo = (x - hi.astype(F32)).astype(BF16)
    return hi, lo


def _mod_kernel(act_ref, w_ref, b_ref, o_ref):
    a = act_ref[...]
    a = (a * jax.nn.sigmoid(a)).astype(BF16)
    o_ref[0] = jnp.dot(a, w_ref[0].astype(BF16), preferred_element_type=F32) + b_ref[0]


def _modulation(act, ada_w, ada_b):
    depth, d, n = ada_w.shape
    tn = 1024
    return pl.pallas_call(
        _mod_kernel,
        grid=(depth, n // tn),
        in_specs=[pl.BlockSpec((16, d), lambda l, j: (0, 0)),
                  pl.BlockSpec((1, d, tn), lambda l, j: (l, 0, j)),
                  pl.BlockSpec((1, 1, tn), lambda l, j: (l, 0, j))],
        out_specs=pl.BlockSpec((1, 16, tn), lambda l, j: (l, 0, j)),
        out_shape=jax.ShapeDtypeStruct((depth, 16, n), F32),
        compiler_params=_cparams(2),
        name="adaln_mod",
    )(act, ada_w, ada_b.reshape(depth, 1, n))


def _is_ctx_row(tm, lc, first_tile=0):
    return lax.broadcasted_iota(jnp.int32, (tm, D_MODEL), 0) < lc - (pl.program_id(0) + first_tile) * tm


def _adaln(x, g, is_ctx, sh_c, sh_b, sc_c, sc_b):
    r = lax.rsqrt(jnp.mean(x * x, axis=-1, keepdims=True) + RMS_EPS)
    gain = jnp.where(is_ctx, g * (1.0 + sc_c[0]), g * (1.0 + sc_b[0]))
    return (x * r) * gain + jnp.where(is_ctx, sh_c[0], sh_b[0])


def _mod_specs(nb, base, comps):
    specs = []
    for comp in comps:
        specs.append(pl.BlockSpec((1, 1, D_MODEL), lambda i, b, comp=comp: (base + nb * 3 + comp, 0, 0)))
        specs.append(pl.BlockSpec((1, 1, D_MODEL), lambda i, b, comp=comp: (base + b * 3 + comp, 0, 0)))
    return specs


def _row_spec(table, row):
    return pl.BlockSpec((1, 1, table.shape[-1]), lambda i, b: (row, 0, 0))


def _ffn_rows(x, is_ctx, sh_c, sh_b, sc_c, sc_b, gt_c, gt_b, g_ref, w1_ref, w2_ref):
    h = _adaln(x, g_ref[0], is_ctx, sh_c, sh_b, sc_c, sc_b).astype(BF16)
    gu = jnp.dot(h, w1_ref[...], preferred_element_type=F32)
    gate = gu[:, :D_FF]
    a = (gate * jax.nn.sigmoid(gate) * gu[:, D_FF:]).astype(BF16)
    y = jnp.dot(a, w2_ref[...], preferred_element_type=F32)
    return x + jnp.where(is_ctx, 0.5 * gt_c[0], 0.5 * gt_b[0]) * y


def _layer_weight(stack, index):
    lead = len(index)
    return pl.BlockSpec((None,) * lead + stack.shape[lead:], lambda i, b: tuple(index) + (0, 0),
                        pipeline_mode=pl.Buffered(1))


def _rope(x, cos, sin, first_half):
    w = x.shape[-1]
    nxt = pltpu.roll(x, w - ROPE_PAIRS, axis=1)
    prv = pltpu.roll(x, ROPE_PAIRS, axis=1)
    return x * cos + jnp.where(first_half, nxt, prv) * sin


def _proj_rows(x, is_ctx, sh_c, sh_b, sc_c, sc_b, g_ref, w_ref, cos_ref, sin_ref,
               pa_ref, pq_ref, pqs_ref, pk_ref, pv_ref, pu_ref):
    h = _adaln(x, g_ref[0], is_ctx, sh_c, sh_b, sc_c, sc_b).astype(BF16)
    p = jnp.dot(h, w_ref[...], preferred_element_type=F32)
    o = A_IN
    pa_ref[...] = p[:, :o]
    cos = cos_ref[...]
    sin = sin_ref[...]
    lane = lax.broadcasted_iota(jnp.int32, (1, B_WIDTH), 1)
    first_half = (lane % (2 * ROPE_PAIRS)) < ROPE_PAIRS
    cos_q = jnp.concatenate([cos] * (B_WIDTH // 128), axis=1)
    sin_q = jnp.concatenate([sin] * (B_WIDTH // 128), axis=1)
    q = _rope(p[:, o:o + B_WIDTH], cos_q, sin_q, first_half)
    q = q * (B_HEAD_DIM ** -0.5 * LOG2E)
    pq_ref[...] = q.astype(BF16)
    low_head = (lane % 128) < B_HEAD_DIM
    q_sw = jnp.where(low_head, pltpu.roll(q, B_WIDTH - B_HEAD_DIM, axis=1), pltpu.roll(q, B_HEAD_DIM, axis=1))
    pqs_ref[...] = q_sw.astype(BF16)
    o += B_WIDTH
    k = _rope(p[:, o:o + B_KV_WIDTH], cos, sin, first_half[:, :B_KV_WIDTH])
    pk_ref[...] = k.astype(BF16)
    o += B_KV_WIDTH
    pv_ref[...] = p[:, o:o + B_KV_WIDTH].astype(BF16)
    o += B_KV_WIDTH
    pu_ref[...] = p[:, o:o + C_WIDTH]


def _head_kernel(*refs, lc, entry):
    if entry:
        ctx_ref, lat_ref, *refs = refs
        x = jnp.where(pl.program_id(0) * ctx_ref.shape[1] < lc, ctx_ref[0], lat_ref[0])
    else:
        x_ref, *refs = refs
        x = x_ref[...]
    ffn_refs, proj_refs, (x_out, *proj_outs) = refs[:9], refs[9:17], refs[17:]
    is_ctx = _is_ctx_row(x.shape[0], lc)
    x = _ffn_rows(x, is_ctx, *ffn_refs)
    x_out[...] = x
    _proj_rows(x, is_ctx, *proj_refs, *proj_outs)


def _layer_head(X, mod, bases, gains, layer, w1s, w2s, w_in_s, cos_t, sin_t, nb, lc):
    d = D_MODEL
    entry = isinstance(X, tuple)
    tm = ROW_TILE if entry else TOKEN_TILE
    tile = lambda w: pl.BlockSpec((tm, w), lambda i, b: (i, b))
    if entry:
        ctx, lat = X
        nct = lc // tm
        tt = lc + lat.shape[1]
        srcs = [pl.BlockSpec((1, tm, d), lambda i, b: (b, jnp.minimum(i, nct - 1), 0)),
                pl.BlockSpec((1, tm, d), lambda i, b: (b, jnp.maximum(i - nct, 0), 0))]
        args, alias = (ctx, lat), {}
    else:
        tt = X.shape[0]
        srcs, args, alias = [tile(d)], (X,), {0: 0}
    widths = (d, A_IN, B_WIDTH, B_WIDTH, B_KV_WIDTH, B_KV_WIDTH, C_WIDTH)
    dtypes = (F32, F32, BF16, BF16, BF16, BF16, F32)
    table = pl.BlockSpec((tm, 128), lambda i, b: (i, 0))
    return pl.pallas_call(
        functools.partial(_head_kernel, lc=lc, entry=entry),
        grid=(tt // tm, nb),
        in_specs=srcs + _mod_specs(nb, bases[0], (0, 1, 2)) + [
                 _row_spec(gains, layer * 3), _layer_weight(w1s, (layer, 0)), _layer_weight(w2s, (layer, 0))]
                 + _mod_specs(nb, bases[1], (0, 1)) + [
                 _row_spec(gains, layer * 3 + 1), _layer_weight(w_in_s, (layer,)), table, table],
        out_specs=[tile(w) for w in widths],
        out_shape=[jax.ShapeDtypeStruct((tt, nb * w), dt) for w, dt in zip(widths, dtypes)],
        input_output_aliases=alias,
        compiler_params=_cparams(2, V7X_VMEM_LIMIT),
        name="layer_head",
    )(*args, *([mod] * 6), gains, w1s, w2s, *([mod] * 4), gains, w_in_s, cos_t, sin_t)


def _hgrn_pairs(lo, hi, reverse):
    if hi - lo <= HGRN_SUB:
        return []
    mid = (lo + hi) // 2
    here = (lo, mid, mid - lo, mid) if reverse else (mid, lo, mid - lo, mid - 1)
    return [here] + _hgrn_pairs(lo, mid, reverse) + _hgrn_pairs(mid, hi, reverse)


def _hgrn_kernel(pa_ref, lb_ref, ng_ref, o_ref,
                 qs_s, kf_s, kb_s, ks_s, bf_s, bb_s, v_s, o_s, p_s, st_s, mask_s, ones_s, tri_s, *, lc):
    tt = pa_ref.shape[0]
    w = A_WIDTH
    ch = HGRN_CHUNK
    sub = HGRN_SUB
    n_sub = ch // sub
    blk = 128
    n_blk = tt // blk
    n_chunks = tt // ch
    n_ctx_chunks = lc // ch
    nt = (((1,), (1,)), ((), ()))
    tn = (((0,), (0,)), ((), ()))

    row = lax.broadcasted_iota(jnp.int32, (w, w), 0)
    col = lax.broadcasted_iota(jnp.int32, (w, w), 1)
    same_head = jnp.where((row // A_HEAD_DIM) == (col // A_HEAD_DIM), 1.0, 0.0)
    mask_s[...] = same_head
    ones_s[...] = same_head.astype(BF16)
    lane = lax.broadcasted_iota(jnp.int32, (1, w), 1)
    head_mask = [jnp.where(lane // A_HEAD_DIM == h, 1.0, 0.0) for h in range(A_HEADS)]
    r = lax.broadcasted_iota(jnp.int32, (blk, blk), 0)
    c = lax.broadcasted_iota(jnp.int32, (blk, blk), 1)
    same_chunk = (r // ch) == (c // ch)
    tri_s[0] = jnp.where(same_chunk, jnp.where(c <= r, 1.0, 0.0), 0.0).astype(BF16)
    tri_s[1] = jnp.where(same_chunk, jnp.where(c >= r, 1.0, 0.0), 0.0).astype(BF16)
    st_s[...] = jnp.zeros(st_s.shape, F32)

    def chunk_cumsum(logf, tri):
        hi = logf.astype(BF16)
        rem = logf - hi.astype(F32)
        mid = rem.astype(BF16)
        lo = (rem - mid.astype(F32)).astype(BF16)
        return (jnp.dot(tri, hi, preferred_element_type=F32)
                + jnp.dot(tri, mid, preferred_element_type=F32)
                + jnp.dot(tri, lo, preferred_element_type=F32))

    def load(ref, rows):
        return jnp.concatenate([ref[c, rows, :] for c in range(w // 128)], axis=1)

    def store(ref, rows, val):
        for c in range(w // 128):
            ref[c, rows, :] = val[:, c * 128:(c + 1) * 128]

    def load_row(ref, r):
        return load(ref, pl.ds(r, sub, stride=0))

    def prep_body(i, carry):
        rows = pl.ds(pl.multiple_of(i * blk, blk), blk)
        q = pa_ref[rows, 0:w]
        qs_s[rows, :] = q * jax.nn.sigmoid(q)
        store(v_s, rows, pa_ref[rows, w:2 * w])
        o_s[rows, :] = jnp.zeros((blk, w), F32)
        k_sum = None
        for d, (k_s, b_s) in enumerate(((kf_s, bf_s), (kb_s, bb_s))):
            lb = lb_ref[d:d + 1, :]
            z = pa_ref[rows, (2 + d) * w:(3 + d) * w]
            t = jnp.exp(-jnp.abs(z))
            big = 1.0 / (1.0 + t)
            small = t * big
            f = lb + (1.0 - lb) * jnp.where(z >= 0, big, small)
            k = (1.0 - lb) * jnp.where(z >= 0, small, big)
            store(k_s, rows, k)
            k_sum = k if k_sum is None else k_sum + k
            store(b_s, rows, chunk_cumsum(jnp.log2(f), tri_s[d]))
        store(ks_s, rows, k_sum)
        return carry
    lax.fori_loop(0, n_blk, prep_body, 0)

    def expand(blocks):
        return jnp.concatenate([x * head_mask[h] for h in range(A_HEADS) for x in blocks], axis=0).astype(BF16)

    def expand_cached(masked, s0, n):
        j0 = s0 // sub
        return jnp.concatenate([masked[j][h] for h in range(A_HEADS) for j in range(j0, j0 + n // sub)],
                               axis=0).astype(BF16)

    def diag_scores(r0):
        t_i = lax.broadcasted_iota(jnp.int32, (sub, w), 0)
        for j in range(n_sub):
            base = r0 + j * sub
            rows = pl.ds(base, sub)
            q = qs_s[rows, :]
            bf = load(bf_s, rows)
            bb = load(bb_s, rows)
            for s in range(sub):
                kf_row = load_row(kf_s, base + s)
                kb_row = load_row(kb_s, base + s)
                arg = jnp.where(t_i >= s, bf - load_row(bf_s, base + s), bb - load_row(bb_s, base + s))
                kk = jnp.where(t_i > s, kf_row, jnp.where(t_i < s, kb_row, load_row(ks_s, base + s)))
                p_s[(j * sub + s) * sub:(j * sub + s + 1) * sub, :] = q * jnp.exp2(arg) * kk
        return jnp.dot(p_s[...].astype(BF16), ones_s[...], preferred_element_type=F32)

    def diag_apply(r0, a):
        out = []
        for j in range(n_sub):
            acc = None
            for s in range(sub):
                v_row = load_row(v_s, r0 + j * sub + s)
                term = a[(j * sub + s) * sub:(j * sub + s + 1) * sub, :] * v_row
                acc = term if acc is None else acc + term
            out.append(acc)
        return out

    def dir_scores(r0, d):
        k_s, b_s = (kf_s, bf_s) if d == 0 else (kb_s, bb_s)
        rows8 = lambda off: pl.ds(r0 + off, sub)
        q_blk = lambda off: qs_s[rows8(off), :]
        b_blk = lambda off: load(b_s, rows8(off))
        k_blk = lambda off: load(k_s, rows8(off))
        offs = lambda start, n: range(start, start + n, sub)
        b_last = load_row(b_s, r0 + ((ch - 1) if d == 0 else 0))
        pairs = _hgrn_pairs(0, ch, d == 1)
        scores = []
        for t0, s0, n, ref in pairs:
            b_ref = load_row(b_s, r0 + ref)
            qx = jnp.concatenate([q_blk(o) * jnp.exp2(b_blk(o) - b_ref) for o in offs(t0, n)], axis=0).astype(BF16)
            kx = expand([k_blk(o) * jnp.exp2(b_ref - b_blk(o)) for o in offs(s0, n)])
            scores.append(lax.dot_general(qx, kx, nt, preferred_element_type=F32))
        st = st_s[d]
        qe = jnp.concatenate([q_blk(o) * jnp.exp2(b_blk(o)) for o in offs(0, ch)], axis=0).astype(BF16)
        kd = jnp.concatenate([k_blk(o) * jnp.exp2(b_last - b_blk(o)) for o in offs(0, ch)], axis=0).astype(BF16)
        o = lax.dot_general(qe, st.astype(BF16), nt, preferred_element_type=F32)
        ut = lax.dot_general(load(v_s, pl.ds(r0, ch)).astype(BF16), kd, tn, preferred_element_type=F32)
        return pairs, scores, o, ut, st, b_last[0:1, :]

    def masked_values(r0):
        blocks = [load(v_s, pl.ds(r0 + j * sub, sub)) for j in range(n_sub)]
        return [[v * head_mask[h] for h in range(A_HEADS)] for v in blocks]

    def dir_values(pairs, scores, v_masked):
        return [jnp.dot(sc.astype(BF16), expand_cached(v_masked, s0, n), preferred_element_type=F32)
                for (t0, s0, n, ref), sc in zip(pairs, scores)]

    def dir_finish(d, pairs, contribs, o, ut, st, b_last):
        st_s[d] = st * jnp.exp2(b_last) + ut * mask_s[...]
        out = [o[j * sub:(j + 1) * sub, :] for j in range(n_sub)]
        for (t0, s0, n, ref), contrib in zip(pairs, contribs):
            for jj in range(n // sub):
                out[t0 // sub + jj] = out[t0 // sub + jj] + contrib[jj * sub:(jj + 1) * sub, :]
        return out

    def chunk_body(i, carry):
        cb = jnp.where(i < n_ctx_chunks, n_ctx_chunks - 1 - i, n_chunks + n_ctx_chunks - 1 - i)
        rf = pl.multiple_of(i * ch, ch)
        rb = pl.multiple_of(cb * ch, ch)
        pf, scf, of, utf, stf, blf = dir_scores(rf, 0)
        pb, scb, ob, utb, stb, blb = dir_scores(rb, 1)
        a = diag_scores(rf)
        cf = dir_values(pf, scf, masked_values(rf))
        cbw = dir_values(pb, scb, masked_values(rb))
        o_fwd = dir_finish(0, pf, cf, of, utf, stf, blf)
        o_bwd = dir_finish(1, pb, cbw, ob, utb, stb, blb)
        o_diag = diag_apply(rf, a)
        rows_f = pl.ds(rf, ch)
        o_s[rows_f, :] = o_s[rows_f, :] + jnp.concatenate([x + y for x, y in zip(o_diag, o_fwd)], axis=0)
        rows_b = pl.ds(rb, ch)
        o_s[rows_b, :] = o_s[rows_b, :] + jnp.concatenate(o_bwd, axis=0)
        return carry
    lax.fori_loop(0, n_chunks, chunk_body, 0)

    def out_body(i, carry):
        rows = pl.ds(pl.multiple_of(i * blk, blk), blk)
        o = o_s[rows, :]
        hi, lo = _split_hi_lo(o * o)
        ones = ones_s[...]
        ms = (jnp.dot(hi, ones, preferred_element_type=F32)
              + jnp.dot(lo, ones, preferred_element_type=F32)) * (1.0 / A_HEAD_DIM)
        g = pa_ref[rows, 4 * w:5 * w]
        o_ref[rows, :] = (o * lax.rsqrt(ms + RMS_EPS) * ng_ref[...] * (g * jax.nn.sigmoid(g))).astype(o_ref.dtype)
        return carry
    lax.fori_loop(0, n_blk, out_body, 0)


def _hgrn(pa, lower_bound, norm_g, nb, lc):
    tt = pa.shape[0]
    w = A_WIDTH
    seq = lambda: pltpu.VMEM((tt, w), F32)
    split = lambda: pltpu.VMEM((w // 128, tt, 128), F32)
    return pl.pallas_call(
        functools.partial(_hgrn_kernel, lc=lc),
        grid=(nb,),
        in_specs=[pl.BlockSpec((tt, A_IN), lambda b: (0, b)),
                  pl.BlockSpec((2, w), lambda b: (0, 0)),
                  pl.BlockSpec((1, w), lambda b: (0, 0))],
        out_specs=pl.BlockSpec((tt, w), lambda b: (0, b)),
        out_shape=jax.ShapeDtypeStruct((tt, nb * w), BF16),
        scratch_shapes=[seq(), split(), split(), split(), split(), split(), split(), seq(),
                        pltpu.VMEM((HGRN_CHUNK * HGRN_SUB, w), F32),
                        pltpu.VMEM((2, w, w), F32), pltpu.VMEM((w, w), F32), pltpu.VMEM((w, w), BF16),
                        pltpu.VMEM((2, 128, 128), BF16)],
        compiler_params=_cparams(1, V7X_VMEM_LIMIT),
        name="hgrn2_mixer",
    )(pa, lower_bound, norm_g.reshape(1, w))


def _attn_kernel(sink_ref, q_ref, qs_ref, k_ref, v_ref, o_ref, kz_s, *, lc):
    tt = k_ref.shape[0]
    blk = ATTN_BLOCK
    hd = B_HEAD_DIM
    j = pl.program_id(1)
    n_ctx_blk = lc // blk
    n_blk = tt // blk
    n_sub = q_ref.shape[0] // blk
    gw = B_GROUP * blk
    nt = (((1,), (1,)), ((), ()))
    tn = (((0,), (0,)), ((), ()))

    @pl.when(j == 0)
    def _():
        lane = lax.broadcasted_iota(jnp.int32, (1, B_KV_WIDTH), 1)
        for g in range(B_KV_HEADS):
            keep = (lane // hd) == g

            def body(i, carry):
                rows = pl.ds(pl.multiple_of(i * blk, blk), blk)
                k = k_ref[rows, :]
                kz_s[g, rows, :] = jnp.where(keep, k, jnp.zeros_like(k))
                return carry
            lax.fori_loop(0, n_blk, body, 0)

    def q_stack(g, qi):
        parts = []
        for hh in range(B_GROUP):
            h = g * B_GROUP + hh
            src = q_ref if (h % 2) == g else qs_ref
            parts.append(src[qi * blk:(qi + 1) * blk, (h // 2) * 128:(h // 2 + 1) * 128])
        return jnp.concatenate(parts, axis=0)

    def scores(g, qi, rows):
        return lax.dot_general(kz_s[g, rows, :], q_stack(g, qi), nt, preferred_element_type=F32)

    def softmax_values(g, pieces):
        sink = jnp.concatenate([jnp.full((1, blk), sink_ref[g * B_GROUP + hh] * LOG2E, F32)
                                for hh in range(B_GROUP)], axis=1)
        m = sink
        for s, _ in pieces:
            m = jnp.maximum(m, jnp.max(s, axis=0, keepdims=True))
        den = jnp.exp2(sink - m)
        probs = []
        for s, _ in pieces:
            p = jnp.exp2(s - m)
            den = den + jnp.sum(p, axis=0, keepdims=True)
            probs.append(p.astype(BF16))
        values = jnp.concatenate([vv for _, vv in pieces], axis=0)
        acc = lax.dot_general(values, jnp.concatenate(probs, axis=0), tn,
                              preferred_element_type=F32)
        return acc[g * hd:(g + 1) * hd, :] * (1.0 / den)

    def write(qi, outs):
        for pair in range(B_Q_HEADS // 2):
            g, hh = (2 * pair) // B_GROUP, (2 * pair) % B_GROUP
            both = jnp.concatenate([outs[g][:, hh * blk:(hh + 1) * blk],
                                    outs[g][:, (hh + 1) * blk:(hh + 2) * blk]], axis=0)
            o_ref[qi * blk:(qi + 1) * blk, pair * 128:(pair + 1) * 128] = both.T.astype(o_ref.dtype)

    def ctx_scores(qi):
        return [[scores(g, qi, pl.ds(0, lc))] for g in range(B_KV_HEADS)]

    def ctx_pieces(qi, sc):
        return [[(sc[g][0], v_ref[0:lc, :])] for g in range(B_KV_HEADS)]

    def band_rows(jb):
        prev = pl.ds(pl.multiple_of((jb - 1) * blk, blk), blk)
        own = pl.ds(pl.multiple_of(jb * blk, blk), blk)
        nxt = pl.ds(pl.multiple_of(jnp.minimum(jb + 1, n_blk - 1) * blk, blk), blk)
        return prev, own, nxt, pl.ds(0, lc)

    def band_scores(qi, jb):
        return [[scores(g, qi, r) for r in band_rows(jb)] for g in range(B_KV_HEADS)]

    def band_pieces(jb, sc):
        key = lax.broadcasted_iota(jnp.int32, (blk, gw), 0)
        qry = lax.broadcasted_iota(jnp.int32, (blk, gw), 1) % blk
        keep_prev = key >= qry + jnp.where(jb > n_ctx_blk, 0, blk)
        keep_next = key <= qry - jnp.where(jb < n_blk - 1, 0, blk)
        masked = MASK_VALUE * LOG2E
        r_prev, r_own, r_next, r_ctx = band_rows(jb)
        out = []
        for g in range(B_KV_HEADS):
            sp, so, sn, sx = sc[g]
            out.append([(jnp.where(keep_prev, sp, masked), v_ref[r_prev, :]), (so, v_ref[r_own, :]),
                        (jnp.where(keep_next, sn, masked), v_ref[r_next, :]), (sx, v_ref[r_ctx, :])])
        return out

    first = j * n_sub

    @pl.when(first < n_ctx_blk)
    def _():
        sc = [ctx_scores(qi) for qi in range(n_sub)]
        for qi in range(n_sub):
            pieces = ctx_pieces(qi, sc[qi])
            write(qi, [softmax_values(g, pieces[g]) for g in range(B_KV_HEADS)])

    @pl.when(first >= n_ctx_blk)
    def _():
        sc = [band_scores(qi, first + qi) for qi in range(n_sub)]
        for qi in range(n_sub):
            pieces = band_pieces(first + qi, sc[qi])
            write(qi, [softmax_values(g, pieces[g]) for g in range(B_KV_HEADS)])


def _attn(pq, pqs, pk, pv, sink, nb, lc):
    tt = pq.shape[0]
    blk = ATTN_Q_BLOCKS * ATTN_BLOCK
    assert lc % blk == 0 and tt % blk == 0
    return pl.pallas_call(
        functools.partial(_attn_kernel, lc=lc),
        grid=(nb, tt // blk),
        in_specs=[pl.BlockSpec(memory_space=pltpu.SMEM),
                  pl.BlockSpec((blk, B_WIDTH), lambda b, j: (j, b)),
                  pl.BlockSpec((blk, B_WIDTH), lambda b, j: (j, b)),
                  pl.BlockSpec((tt, B_KV_WIDTH), lambda b, j: (0, b)),
                  pl.BlockSpec((tt, B_KV_WIDTH), lambda b, j: (0, b))],
        out_specs=pl.BlockSpec((blk, B_WIDTH), lambda b, j: (j, b)),
        out_shape=jax.ShapeDtypeStruct((tt, nb * B_WIDTH), BF16),
        scratch_shapes=[pltpu.VMEM((B_KV_HEADS, tt, B_KV_WIDTH), BF16)],
        compiler_params=_cparams(2),
        name="window_gqa",
    )(sink, pq, pqs, pk, pv)


def _s5_tile_index(reverse, i, n_ctx, n_all):
    if not reverse:
        return i
    return jnp.where(i < n_ctx, n_ctx - 1 - i, n_all + n_ctx - 1 - i)


def _s5_kernel(u_ref, a_ref, wb_ref, wc_ref, y_ref, u_s, y_s, h_s, *, nb, reverse):
    i = pl.program_id(0)
    n = C_NSTATE
    steps = u_ref.shape[0]
    halves = C_WIDTH // 128
    group = S5_GROUP
    n_groups = steps // group

    @pl.when(i == 0)
    def _():
        h_s[...] = jnp.zeros(h_s.shape, F32)

    for b in range(nb):
        for c in range(halves):
            lanes = slice(b * C_WIDTH + c * 128, b * C_WIDTH + (c + 1) * 128)
            u_s[c, pl.ds(b, steps, stride=nb), :] = u_ref[:, lanes]
    a_re = jnp.broadcast_to(a_ref[0:1, :], (nb, n))
    a_im = jnp.broadcast_to(a_ref[1:2, :], (nb, n))

    def drive(g):
        rows = slice(g * group * nb, (g + 1) * group * nb)
        u = jnp.concatenate([u_s[c, rows, :] for c in range(halves)], axis=1)
        return jnp.dot(u.astype(BF16), wb_ref[...], preferred_element_type=F32)

    order = list(range(n_groups))[::-1] if reverse else list(range(n_groups))
    lookahead = 2
    driven = {g: drive(g) for g in order[:lookahead]}
    h_re, h_im = h_s[:, 0:n], h_s[:, n:2 * n]
    for pos, g in enumerate(order):
        bu = driven.pop(g)
        states = [None] * group
        for t in (range(group - 1, -1, -1) if reverse else range(group)):
            slab = bu[t * nb:(t + 1) * nb, :]
            h_re, h_im = (a_re * h_re - a_im * h_im + slab[:, 0:n],
                          a_re * h_im + a_im * h_re + slab[:, n:2 * n])
            states[t] = jnp.concatenate([h_re, h_im], axis=1)
        if pos + lookahead < n_groups:
            nxt = order[pos + lookahead]
            driven[nxt] = drive(nxt)
        hb = jnp.concatenate(states, axis=0).astype(BF16)
        y = jnp.dot(hb, wc_ref[...], preferred_element_type=F32)
        rows = slice(g * group * nb, (g + 1) * group * nb)
        for c in range(halves):
            y_s[c, rows, :] = y[:, c * 128:(c + 1) * 128]
    h_s[:, 0:n] = h_re
    h_s[:, n:2 * n] = h_im
    for b in range(nb):
        for c in range(halves):
            lanes = slice(b * C_WIDTH + c * 128, b * C_WIDTH + (c + 1) * 128)
            y_ref[:, lanes] = y_s[c, pl.ds(b, steps, stride=nb), :]


def _s5_scan(pu, a, wb, wc, index, nb, lc, reverse):
    tt = pu.shape[0]
    ts = S5_TILE
    n_all = tt // ts
    n_ctx = lc // ts
    tile = lambda i: (_s5_tile_index(reverse, i, n_ctx, n_all), 0)
    split = lambda: pltpu.VMEM((C_WIDTH // 128, ts * nb, 128), F32)
    const = lambda stack: pl.BlockSpec((None, None) + stack.shape[2:], lambda i: tuple(index) + (0, 0))
    return pl.pallas_call(
        functools.partial(_s5_kernel, nb=nb, reverse=reverse),
        grid=(n_all,),
        in_specs=[pl.BlockSpec((ts, nb * C_WIDTH), tile), const(a), const(wb), const(wc)],
        out_specs=pl.BlockSpec((ts, nb * C_WIDTH), tile),
        out_shape=jax.ShapeDtypeStruct((tt, nb * C_WIDTH), F32),
        scratch_shapes=[split(), split(), pltpu.VMEM((nb, 2 * C_NSTATE), F32)],
        compiler_params=_cparams(1, V7X_VMEM_LIMIT),
        name="s5_scan",
    )(pu, a, wb, wc)


def _s5_params(a_re, a_im, log_dt, b_re, b_im, c_re, c_im):
    eye = jnp.eye(C_GROUPS, dtype=F32)
    dt = jnp.exp(log_dt)[..., None]
    mag = jnp.exp(a_re * dt)
    ang = a_im * dt
    abar_re, abar_im = mag * jnp.cos(ang), mag * jnp.sin(ang)
    den = a_re * a_re + a_im * a_im
    coef_re = ((abar_re - 1.0) * a_re + abar_im * a_im) / den
    coef_im = (abar_im * a_re - (abar_re - 1.0) * a_im) / den
    b_re, b_im = b_re[:, None], b_im[:, None]
    bbar_re = coef_re[..., None] * b_re - coef_im[..., None] * b_im
    bbar_im = coef_re[..., None] * b_im + coef_im[..., None] * b_re
    lead = a_re.shape[:2]
    a = jnp.stack([abar_re.reshape(lead + (C_NSTATE,)), abar_im.reshape(lead + (C_NSTATE,))], axis=2)
    drive = lambda bb: jnp.einsum('lkgpc,gh->lkgchp', bb, eye).reshape(lead + (C_WIDTH, C_NSTATE))
    read = lambda cc: jnp.einsum('lkgcp,gh->lkgphc', cc, eye).reshape(lead + (C_NSTATE, C_WIDTH))
    wb = jnp.concatenate([drive(bbar_re), drive(bbar_im)], axis=-1)
    wc = jnp.concatenate([read(c_re), -read(c_im)], axis=-2)
    return a, wb.astype(BF16), wc.astype(BF16)


def _out_rows(x, is_ctx, gt_c, gt_b, a_ref, b_ref, u_ref, yf_ref, yb_ref, d_ref, gw_ref, gb_ref, wo_ref):
    y = d_ref[...] * u_ref[...] + yf_ref[...] + yb_ref[...]
    gelu = 0.5 * y * (1.0 + jnp.tanh(math.sqrt(2.0 / math.pi) * (y + 0.044715 * (y * y * y))))
    hg = jnp.dot(gelu.astype(BF16), gw_ref[...], preferred_element_type=F32) + gb_ref[...]
    c = hg[:, :C_WIDTH] * jax.nn.sigmoid(hg[:, C_WIDTH:])
    mix = (jnp.dot(a_ref[...], wo_ref[0:A_WIDTH, :], preferred_element_type=F32)
           + jnp.dot(b_ref[...], wo_ref[A_WIDTH:A_WIDTH + B_WIDTH, :], preferred_element_type=F32)
           + jnp.dot(c.astype(BF16), wo_ref[A_WIDTH + B_WIDTH:, :], preferred_element_type=F32))
    return x + jnp.where(is_ctx, gt_c[0], gt_b[0]) * mix


def _tail_kernel(x_ref, *refs, lc, final):
    x = x_ref[...]
    tm = x.shape[0]
    is_ctx = _is_ctx_row(tm, lc, lc // tm if final else 0)
    x = _out_rows(x, is_ctx, *refs[:11])
    y = _ffn_rows(x, is_ctx, *refs[11:20])
    if final:
        fg_ref, o_ref = refs[20:]
        o_ref[0] = y * lax.rsqrt(jnp.mean(y * y, axis=-1, keepdims=True) + RMS_EPS) * fg_ref[...]
    else:
        refs[20][...] = y


def _layer_tail(X, mod, bases, gains, layer, a_o, b_o, pu, y2, s5_d, glu_w_s, glu_b, w_out_s, w1s, w2s, nb, lc,
                final_g=None):
    tt = X.shape[0]
    d = D_MODEL
    final = final_g is not None
    tm = ROW_TILE if final else TOKEN_TILE
    first = lc // tm if final else 0
    const = lambda shape: pl.BlockSpec(shape, lambda i, b: (0,) * len(shape))
    tile = lambda w: pl.BlockSpec((tm, w), lambda i, b: (i + first, b))
    extra_specs, extra, alias = [], (), {0: 0}
    out_spec, out_shape = tile(d), (tt, nb * d)
    if final:
        extra_specs, extra, alias = [const((1, d))], (final_g.reshape(1, d),), {}
        out_spec, out_shape = pl.BlockSpec((1, tm, d), lambda i, b: (b, i, 0)), (nb, tt - lc, d)
    return pl.pallas_call(
        functools.partial(_tail_kernel, lc=lc, final=final),
        grid=(tt // tm - first, nb),
        in_specs=[tile(d)] + _mod_specs(nb, bases[1], (2,)) + [
                 tile(A_WIDTH), tile(B_WIDTH), tile(C_WIDTH), tile(C_WIDTH), tile(C_WIDTH),
                 const((1, C_WIDTH)), _layer_weight(glu_w_s, (layer,)), const((1, 2 * C_WIDTH)),
                 _layer_weight(w_out_s, (layer,))]
                 + _mod_specs(nb, bases[2], (0, 1, 2)) + [
                 _row_spec(gains, layer * 3 + 2), _layer_weight(w1s, (layer, 1)), _layer_weight(w2s, (layer, 1))]
                 + extra_specs,
        out_specs=out_spec,
        out_shape=jax.ShapeDtypeStruct(out_shape, F32),
        input_output_aliases=alias,
        compiler_params=_cparams(2, V7X_VMEM_LIMIT),
        name="layer_tail",
    )(X, mod, mod, a_o, b_o, pu, y2[0], y2[1], s5_d.reshape(1, C_WIDTH), glu_w_s, glu_b.reshape(1, 2 * C_WIDTH),
      w_out_s, *([mod] * 6), gains, w1s, w2s, *extra)


def _rope_tables(length, lc):
    t = jnp.arange(length)
    pos = jnp.stack([(t // GRID_W).astype(F32), (t % GRID_W).astype(F32)], axis=1)
    inv_freq = ROPE_BASE ** (-jnp.arange(ROPE_PAIRS, dtype=F32) / ROPE_PAIRS)
    ang = pos[:, :, None] * inv_freq
    cos = jnp.cos(ang)[:, :, None, :]
    sin = jnp.sin(ang)[:, :, None, :]
    cos = jnp.broadcast_to(cos, (length, 2, 2, ROPE_PAIRS)).reshape(length, B_HEAD_DIM)
    sin = jnp.concatenate([-sin, sin], axis=2).reshape(length, B_HEAD_DIM)
    cos = jnp.concatenate([jnp.ones((lc, B_HEAD_DIM), F32), cos], axis=0)
    sin = jnp.concatenate([jnp.zeros((lc, B_HEAD_DIM), F32), sin], axis=0)
    return jnp.tile(cos, (1, 128 // B_HEAD_DIM)), jnp.tile(sin, (1, 128 // B_HEAD_DIM))


def kernel(x, c, ctx, c_ctx, ada_w, ada_b, norm_g, ffn_w1, ffn_w2, w_in, w_out, hgrn_lower_bounds,
           hgrn_norm_g, attn_sink, s5_a_re, s5_a_im, s5_log_dt, s5_b_re, s5_b_im, s5_c_re, s5_c_im,
           s5_d, s5_glu_w, s5_glu_b, final_norm_g):
    nb, length, d = x.shape
    lc = ctx.shape[1]
    tt = lc + length
    depth = ada_w.shape[0]
    assert d == D_MODEL and nb == 8 and lc % ROW_TILE == 0 and length % ROW_TILE == 0 and tt % TOKEN_TILE == 0

    act = jnp.concatenate([c, c_ctx[None], jnp.zeros((16 - nb - 1, d), F32)], axis=0)
    mod = _modulation(act, ada_w, ada_b).reshape(depth, 16, 3, 3, d)[:, :nb + 1]
    mod = mod.transpose(0, 2, 1, 3, 4).reshape(depth * 3 * (nb + 1) * 3, 1, d)
    cos_t, sin_t = _rope_tables(length, lc)
    lb_soft = jax.nn.softmax(hgrn_lower_bounds.astype(F32), axis=0)
    lower_bound = jnp.cumsum(lb_soft, axis=0) - lb_soft[0]

    w1s, w2s = ffn_w1.astype(BF16), ffn_w2.astype(BF16)
    w_in_s, w_out_s, glu_w_s = w_in.astype(BF16), w_out.astype(BF16), s5_glu_w.astype(BF16)
    s5a, s5wb, s5wc = _s5_params(s5_a_re, s5_a_im, s5_log_dt, s5_b_re, s5_b_im, s5_c_re, s5_c_im)
    gains = norm_g.reshape(depth * 3, 1, d)
    X = (ctx, x)
    for l in range(depth):
        base = [(l * 3 + sub) * (nb + 1) * 3 for sub in range(3)]
        X, pa, pq, pqs, pk, pv, pu = _layer_head(X, mod, base, gains, l, w1s, w2s, w_in_s, cos_t, sin_t, nb, lc)
        a_o = _hgrn(pa, lower_bound[l], hgrn_norm_g[l], nb, lc)
        b_o = _attn(pq, pqs, pk, pv, attn_sink[l], nb, lc)
        y2 = [_s5_scan(pu, s5a, s5wb, s5wc, (l, k), nb, lc, k == 1) for k in range(2)]
        X = _layer_tail(X, mod, base, gains, l, a_o, b_o, pu, y2, s5_d[l], glu_w_s, s5_glu_b[l], w_out_s, w1s, w2s,
                        nb, lc, final_g=final_norm_g if l == depth - 1 else None)
    return X
```

```python
import functools
import math

import jax
import jax.numpy as jnp
from jax import lax
from jax.experimental import pallas as pl
from jax.experimental.pallas import tpu as pltpu

F32 = jnp.float32
BF16 = jnp.bfloat16

D_MODEL = 1024
DEPTH = 4
GRID_W = 64
RMS_EPS = 1e-6
N_MOD = 9
D_FF = 2816
A_HEADS = 4
A_HEAD_DIM = 64
A_WIDTH = A_HEADS * A_HEAD_DIM
GLR_CHUNK = 16
B_Q_HEADS = 8
B_KV_HEADS = 2
B_GROUP = B_Q_HEADS // B_KV_HEADS
B_HEAD_DIM = 64
B_WIDTH = B_Q_HEADS * B_HEAD_DIM
B_KV_WIDTH = B_KV_HEADS * B_HEAD_DIM
WINDOW = 128
ATTN_BLOCK = 128
ATTN_Q_BLOCKS = 2
ROPE_BASE = 10000.0
ROPE_PAIRS = B_HEAD_DIM // 4
MASK_VALUE = -1e9
LOG2E = math.log2(math.e)
C_GROUPS = 16
C_GROUP_CH = 16
C_WIDTH = C_GROUPS * C_GROUP_CH
C_STATE = 64
C_NSTATE = C_GROUPS * C_STATE
D_MIX = A_WIDTH + B_WIDTH + C_WIDTH
A_IN = 5 * A_WIDTH
D_IN = A_IN + B_WIDTH + 2 * B_KV_WIDTH + C_WIDTH

V7X_VMEM_LIMIT = 56 * 1024 * 1024
ROW_TILE = 256
TOKEN_TILE = 576
S5_TILE = 128
S5_GROUP = 16
HGRN_CHUNK = 128
HGRN_SUB = 8


def _cparams(n_axes, vmem=None):
    return pltpu.CompilerParams(dimension_semantics=("arbitrary",) * n_axes, vmem_limit_bytes=vmem)


def _split_hi_lo(x):
    hi = x.astype(BF16)
    lo = (x - hi.astype(F32)).astype(BF16)
    return hi, lo


def _mod_kernel(act_ref, w_ref, b_ref, o_ref):
    a = act_ref[...]
    a = (a * jax.nn.sigmoid(a)).astype(BF16)
    o_ref[0] = jnp.dot(a, w_ref[0].astype(BF16), preferred_element_type=F32) + b_ref[0]


def _modulation(act, ada_w, ada_b):
    depth, d, n = ada_w.shape
    tn = 1024
    return pl.pallas_call(
        _mod_kernel,
        grid=(depth, n // tn),
        in_specs=[pl.BlockSpec((16, d), lambda l, j: (0, 0)),
                  pl.BlockSpec((1, d, tn), lambda l, j: (l, 0, j)),
                  pl.BlockSpec((1, 1, tn), lambda l, j: (l, 0, j))],
        out_specs=pl.BlockSpec((1, 16, tn), lambda l, j: (l, 0, j)),
        out_shape=jax.ShapeDtypeStruct((depth, 16, n), F32),
        compiler_params=_cparams(2),
        name="adaln_mod",
    )(act, ada_w, ada_b.reshape(depth, 1, n))


def _is_ctx_row(tm, lc, first_tile=0):
    return lax.broadcasted_iota(jnp.int32, (tm, D_MODEL), 0) < lc - (pl.program_id(0) + first_tile) * tm


def _adaln(x, g, is_ctx, sh_c, sh_b, sc_c, sc_b):
    r = lax.rsqrt(jnp.mean(x * x, axis=-1, keepdims=True) + RMS_EPS)
    gain = jnp.where(is_ctx, g * (1.0 + sc_c[0]), g * (1.0 + sc_b[0]))
    return (x * r) * gain + jnp.where(is_ctx, sh_c[0], sh_b[0])


def _mod_specs(nb, base, comps):
    specs = []
    for comp in comps:
        specs.append(pl.BlockSpec((1, 1, D_MODEL), lambda i, b, comp=comp: (base + nb * 3 + comp, 0, 0)))
        specs.append(pl.BlockSpec((1, 1, D_MODEL), lambda i, b, comp=comp: (base + b * 3 + comp, 0, 0)))
    return specs


def _row_spec(table, row):
    return pl.BlockSpec((1, 1, table.shape[-1]), lambda i, b: (row, 0, 0))


def _ffn_rows(x, is_ctx, sh_c, sh_b, sc_c, sc_b, gt_c, gt_b, g_ref, w1_ref, w2_ref):
    h = _adaln(x, g_ref[0], is_ctx, sh_c, sh_b, sc_c, sc_b).astype(BF16)
    gu = jnp.dot(h, w1_ref[...], preferred_element_type=F32)
    gate = gu[:, :D_FF]
    a = (gate * jax.nn.sigmoid(gate) * gu[:, D_FF:]).astype(BF16)
    y = jnp.dot(a, w2_ref[...], preferred_element_type=F32)
    return x + jnp.where(is_ctx, 0.5 * gt_c[0], 0.5 * gt_b[0]) * y


def _layer_weight(stack, index):
    lead = len(index)
    return pl.BlockSpec((None,) * lead + stack.shape[lead:], lambda i, b: tuple(index) + (0, 0),
                        pipeline_mode=pl.Buffered(1))


def _rope(x, cos, sin, first_half):
    w = x.shape[-1]
    nxt = pltpu.roll(x, w - ROPE_PAIRS, axis=1)
    prv = pltpu.roll(x, ROPE_PAIRS, axis=1)
    return x * cos + jnp.where(first_half, nxt, prv) * sin


def _proj_rows(x, is_ctx, sh_c, sh_b, sc_c, sc_b, g_ref, w_ref, cos_ref, sin_ref,
               pa_ref, pq_ref, pqs_ref, pk_ref, pv_ref, pu_ref):
    h = _adaln(x, g_ref[0], is_ctx, sh_c, sh_b, sc_c, sc_b).astype(BF16)
    p = jnp.dot(h, w_ref[...], preferred_element_type=F32)
    o = A_IN
    pa_ref[...] = p[:, :o]
    cos = cos_ref[...]
    sin = sin_ref[...]
    lane = lax.broadcasted_iota(jnp.int32, (1, B_WIDTH), 1)
    first_half = (lane % (2 * ROPE_PAIRS)) < ROPE_PAIRS
    cos_q = jnp.concatenate([cos] * (B_WIDTH // 128), axis=1)
    sin_q = jnp.concatenate([sin] * (B_WIDTH // 128), axis=1)
    q = _rope(p[:, o:o + B_WIDTH], cos_q, sin_q, first_half)
    q = q * (B_HEAD_DIM ** -0.5 * LOG2E)
    pq_ref[...] = q.astype(BF16)
    low_head = (lane % 128) < B_HEAD_DIM
    q_sw = jnp.where(low_head, pltpu.roll(q, B_WIDTH - B_HEAD_DIM, axis=1), pltpu.roll(q, B_HEAD_DIM, axis=1))
    pqs_ref[...] = q_sw.astype(BF16)
    o += B_WIDTH
    k = _rope(p[:, o:o + B_KV_WIDTH], cos, sin, first_half[:, :B_KV_WIDTH])
    pk_ref[...] = k.astype(BF16)
    o += B_KV_WIDTH
    pv_ref[...] = p[:, o:o + B_KV_WIDTH].astype(BF16)
    o += B_KV_WIDTH
    pu_ref[...] = p[:, o:o + C_WIDTH]


def _head_kernel(*refs, lc, entry):
    if entry:
        ctx_ref, lat_ref, *refs = refs
        x = jnp.where(pl.program_id(0) * ctx_ref.shape[1] < lc, ctx_ref[0], lat_ref[0])
    else:
        x_ref, *refs = refs
        x = x_ref[...]
    ffn_refs, proj_refs, (x_out, *proj_outs) = refs[:9], refs[9:17], refs[17:]
    is_ctx = _is_ctx_row(x.shape[0], lc)
    x = _ffn_rows(x, is_ctx, *ffn_refs)
    x_out[...] = x
    _proj_rows(x, is_ctx, *proj_refs, *proj_outs)


def _layer_head(X, mod, bases, gains, layer, w1s, w2s, w_in_s, cos_t, sin_t, nb, lc):
    d = D_MODEL
    entry = isinstance(X, tuple)
    tm = ROW_TILE if entry else TOKEN_TILE
    tile = lambda w: pl.BlockSpec((tm, w), lambda i, b: (i, b))
    if entry:
        ctx, lat = X
        nct = lc // tm
        tt = lc + lat.shape[1]
        srcs = [pl.BlockSpec((1, tm, d), lambda i, b: (b, jnp.minimum(i, nct - 1), 0)),
                pl.BlockSpec((1, tm, d), lambda i, b: (b, jnp.maximum(i - nct, 0), 0))]
        args, alias = (ctx, lat), {}
    else:
        tt = X.shape[0]
        srcs, args, alias = [tile(d)], (X,), {0: 0}
    widths = (d, A_IN, B_WIDTH, B_WIDTH, B_KV_WIDTH, B_KV_WIDTH, C_WIDTH)
    dtypes = (F32, F32, BF16, BF16, BF16, BF16, F32)
    table = pl.BlockSpec((tm, 128), lambda i, b: (i, 0))
    return pl.pallas_call(
        functools.partial(_head_kernel, lc=lc, entry=entry),
        grid=(tt // tm, nb),
        in_specs=srcs + _mod_specs(nb, bases[0], (0, 1, 2)) + [
                 _row_spec(gains, layer * 3), _layer_weight(w1s, (layer, 0)), _layer_weight(w2s, (layer, 0))]
                 + _mod_specs(nb, bases[1], (0, 1)) + [
                 _row_spec(gains, layer * 3 + 1), _layer_weight(w_in_s, (layer,)), table, table],
        out_specs=[tile(w) for w in widths],
        out_shape=[jax.ShapeDtypeStruct((tt, nb * w), dt) for w, dt in zip(widths, dtypes)],
        input_output_aliases=alias,
        compiler_params=_cparams(2, V7X_VMEM_LIMIT),
        name="layer_head",
    )(*args, *([mod] * 6), gains, w1s, w2s, *([mod] * 4), gains, w_in_s, cos_t, sin_t)


def _hgrn_pairs(lo, hi, reverse):
    if hi - lo <= HGRN_SUB:
        return []
    mid = (lo + hi) // 2
    here = (lo, mid, mid - lo, mid) if reverse else (mid, lo, mid - lo, mid - 1)
    return [here] + _hgrn_pairs(lo, mid, reverse) + _hgrn_pairs(mid, hi, reverse)


def _hgrn_kernel(pa_ref, lb_ref, ng_ref, o_ref,
                 qs_s, kf_s, kb_s, ks_s, bf_s, bb_s, v_s, o_s, p_s, st_s, mask_s, ones_s, tri_s, *, lc):
    tt = pa_ref.shape[0]
    w = A_WIDTH
    ch = HGRN_CHUNK
    sub = HGRN_SUB
    n_sub = ch // sub
    blk = 128
    n_blk = tt // blk
    n_chunks = tt // ch
    n_ctx_chunks = lc // ch
    nt = (((1,), (1,)), ((), ()))
    tn = (((0,), (0,)), ((), ()))

    row = lax.broadcasted_iota(jnp.int32, (w, w), 0)
    col = lax.broadcasted_iota(jnp.int32, (w, w), 1)
    same_head = jnp.where((row // A_HEAD_DIM) == (col // A_HEAD_DIM), 1.0, 0.0)
    mask_s[...] = same_head
    ones_s[...] = same_head.astype(BF16)
    lane = lax.broadcasted_iota(jnp.int32, (1, w), 1)
    head_mask = [jnp.where(lane // A_HEAD_DIM == h, 1.0, 0.0) for h in range(A_HEADS)]
    r = lax.broadcasted_iota(jnp.int32, (blk, blk), 0)
    c = lax.broadcasted_iota(jnp.int32, (blk, blk), 1)
    same_chunk = (r // ch) == (c // ch)
    tri_s[0] = jnp.where(same_chunk, jnp.where(c <= r, 1.0, 0.0), 0.0).astype(BF16)
    tri_s[1] = jnp.where(same_chunk, jnp.where(c >= r, 1.0, 0.0), 0.0).astype(BF16)
    st_s[...] = jnp.zeros(st_s.shape, F32)

    def chunk_cumsum(logf, tri):
        hi = logf.astype(BF16)
        rem = logf - hi.astype(F32)
        mid = rem.astype(BF16)
        lo = (rem - mid.astype(F32)).astype(BF16)
        return (jnp.dot(tri, hi, preferred_element_type=F32)
                + jnp.dot(tri, mid, preferred_element_type=F32)
                + jnp.dot(tri, lo, preferred_element_type=F32))

    def load(ref, rows):
        return jnp.concatenate([ref[c, rows, :] for c in range(w // 128)], axis=1)

    def store(ref, rows, val):
        for c in range(w // 128):
            ref[c, rows, :] = val[:, c * 128:(c + 1) * 128]

    def load_row(ref, r):
        return load(ref, pl.ds(r, sub, stride=0))

    def prep_body(i, carry):
        rows = pl.ds(pl.multiple_of(i * blk, blk), blk)
        q = pa_ref[rows, 0:w]
        qs_s[rows, :] = q * jax.nn.sigmoid(q)
        store(v_s, rows, pa_ref[rows, w:2 * w])
        o_s[rows, :] = jnp.zeros((blk, w), F32)
        k_sum = None
        for d, (k_s, b_s) in enumerate(((kf_s, bf_s), (kb_s, bb_s))):
            lb = lb_ref[d:d + 1, :]
            z = pa_ref[rows, (2 + d) * w:(3 + d) * w]
            t = jnp.exp(-jnp.abs(z))
            big = 1.0 / (1.0 + t)
            small = t * big
            f = lb + (1.0 - lb) * jnp.where(z >= 0, big, small)
            k = (1.0 - lb) * jnp.where(z >= 0, small, big)
            store(k_s, rows, k)
            k_sum = k if k_sum is None else k_sum + k
            store(b_s, rows, chunk_cumsum(jnp.log2(f), tri_s[d]))
        store(ks_s, rows, k_sum)
        return carry
    lax.fori_loop(0, n_blk, prep_body, 0, unroll=3)

    def expand(blocks):
        return jnp.concatenate([x * head_mask[h] for h in range(A_HEADS) for x in blocks], axis=0).astype(BF16)

    def expand_cached(masked, s0, n):
        j0 = s0 // sub
        return jnp.concatenate([masked[j][h] for h in range(A_HEADS) for j in range(j0, j0 + n // sub)],
                               axis=0).astype(BF16)

    def diag_scores(r0):
        t_i = lax.broadcasted_iota(jnp.int32, (sub, w), 0)
        for j in range(n_sub):
            base = r0 + j * sub
            rows = pl.ds(base, sub)
            q = qs_s[rows, :]
            bf = load(bf_s, rows)
            bb = load(bb_s, rows)
            for s in range(sub):
                kf_row = load_row(kf_s, base + s)
                kb_row = load_row(kb_s, base + s)
                arg = jnp.where(t_i >= s, bf - load_row(bf_s, base + s), bb - load_row(bb_s, base + s))
                kk = jnp.where(t_i > s, kf_row, jnp.where(t_i < s, kb_row, load_row(ks_s, base + s)))
                p_s[(j * sub + s) * sub:(j * sub + s + 1) * sub, :] = q * jnp.exp2(arg) * kk
        return jnp.dot(p_s[...].astype(BF16), ones_s[...], preferred_element_type=F32)

    def diag_apply(r0, a):
        out = []
        for j in range(n_sub):
            acc = None
            for s in range(sub):
                v_row = load_row(v_s, r0 + j * sub + s)
                term = a[(j * sub + s) * sub:(j * sub + s + 1) * sub, :] * v_row
                acc = term if acc is None else acc + term
            out.append(acc)
        return out

    def dir_scores(r0, d):
        k_s, b_s = (kf_s, bf_s) if d == 0 else (kb_s, bb_s)
        rows8 = lambda off: pl.ds(r0 + off, sub)
        q_blk = lambda off: qs_s[rows8(off), :]
        b_blk = lambda off: load(b_s, rows8(off))
        k_blk = lambda off: load(k_s, rows8(off))
        offs = lambda start, n: range(start, start + n, sub)
        b_last = load_row(b_s, r0 + ((ch - 1) if d == 0 else 0))
        pairs = _hgrn_pairs(0, ch, d == 1)
        scores = []
        for t0, s0, n, ref in pairs:
            b_ref = load_row(b_s, r0 + ref)
            qx = jnp.concatenate([q_blk(o) * jnp.exp2(b_blk(o) - b_ref) for o in offs(t0, n)], axis=0).astype(BF16)
            kx = expand([k_blk(o) * jnp.exp2(b_ref - b_blk(o)) for o in offs(s0, n)])
            scores.append(lax.dot_general(qx, kx, nt, preferred_element_type=F32))
        st = st_s[d]
        qe = jnp.concatenate([q_blk(o) * jnp.exp2(b_blk(o)) for o in offs(0, ch)], axis=0).astype(BF16)
        kd = jnp.concatenate([k_blk(o) * jnp.exp2(b_last - b_blk(o)) for o in offs(0, ch)], axis=0).astype(BF16)
        o = lax.dot_general(qe, st.astype(BF16), nt, preferred_element_type=F32)
        ut = lax.dot_general(load(v_s, pl.ds(r0, ch)).astype(BF16), kd, tn, preferred_element_type=F32)
        return pairs, scores, o, ut, st, b_last[0:1, :]

    def masked_values(r0):
        blocks = [load(v_s, pl.ds(r0 + j * sub, sub)) for j in range(n_sub)]
        return [[v * head_mask[h] for h in range(A_HEADS)] for v in blocks]

    def dir_values(pairs, scores, v_masked):
        return [jnp.dot(sc.astype(BF16), expand_cached(v_masked, s0, n), preferred_element_type=F32)
                for (t0, s0, n, ref), sc in zip(pairs, scores)]

    def dir_finish(d, pairs, contribs, o, ut, st, b_last):
        st_s[d] = st * jnp.exp2(b_last) + ut * mask_s[...]
        out = [o[j * sub:(j + 1) * sub, :] for j in range(n_sub)]
        for (t0, s0, n, ref), contrib in zip(pairs, contribs):
            for jj in range(n // sub):
                out[t0 // sub + jj] = out[t0 // sub + jj] + contrib[jj * sub:(jj + 1) * sub, :]
        return out

    def chunk_body(i, carry):
        cb = jnp.where(i < n_ctx_chunks, n_ctx_chunks - 1 - i, n_chunks + n_ctx_chunks - 1 - i)
        rf = pl.multiple_of(i * ch, ch)
        rb = pl.multiple_of(cb * ch, ch)
        pf, scf, of, utf, stf, blf = dir_scores(rf, 0)
        pb, scb, ob, utb, stb, blb = dir_scores(rb, 1)
        a = diag_scores(rf)
        cf = dir_values(pf, scf, masked_values(rf))
        cbw = dir_values(pb, scb, masked_values(rb))
        o_fwd = dir_finish(0, pf, cf, of, utf, stf, blf)
        o_bwd = dir_finish(1, pb, cbw, ob, utb, stb, blb)
        o_diag = diag_apply(rf, a)
        rows_f = pl.ds(rf, ch)
        o_s[rows_f, :] = o_s[rows_f, :] + jnp.concatenate([x + y for x, y in zip(o_diag, o_fwd)], axis=0)
        rows_b = pl.ds(rb, ch)
        o_s[rows_b, :] = o_s[rows_b, :] + jnp.concatenate(o_bwd, axis=0)
        return carry
    lax.fori_loop(0, n_chunks, chunk_body, 0, unroll=2)

    def out_body(i, carry):
        rows = pl.ds(pl.multiple_of(i * blk, blk), blk)
        o = o_s[rows, :]
        hi, lo = _split_hi_lo(o * o)
        ones = ones_s[...]
        ms = (jnp.dot(hi, ones, preferred_element_type=F32)
              + jnp.dot(lo, ones, preferred_element_type=F32)) * (1.0 / A_HEAD_DIM)
        g = pa_ref[rows, 4 * w:5 * w]
        o_ref[rows, :] = (o * lax.rsqrt(ms + RMS_EPS) * ng_ref[...] * (g * jax.nn.sigmoid(g))).astype(o_ref.dtype)
        return carry
    lax.fori_loop(0, n_blk, out_body, 0, unroll=3)


def _hgrn(pa, lower_bound, norm_g, nb, lc):
    tt = pa.shape[0]
    w = A_WIDTH
    seq = lambda: pltpu.VMEM((tt, w), F32)
    split = lambda: pltpu.VMEM((w // 128, tt, 128), F32)
    return pl.pallas_call(
        functools.partial(_hgrn_kernel, lc=lc),
        grid=(nb,),
        in_specs=[pl.BlockSpec((tt, A_IN), lambda b: (0, b)),
                  pl.BlockSpec((2, w), lambda b: (0, 0)),
                  pl.BlockSpec((1, w), lambda b: (0, 0))],
        out_specs=pl.BlockSpec((tt, w), lambda b: (0, b)),
        out_shape=jax.ShapeDtypeStruct((tt, nb * w), BF16),
        scratch_shapes=[seq(), split(), split(), split(), split(), split(), split(), seq(),
                        pltpu.VMEM((HGRN_CHUNK * HGRN_SUB, w), F32),
                        pltpu.VMEM((2, w, w), F32), pltpu.VMEM((w, w), F32), pltpu.VMEM((w, w), BF16),
                        pltpu.VMEM((2, 128, 128), BF16)],
        compiler_params=_cparams(1, V7X_VMEM_LIMIT),
        name="hgrn2_mixer",
    )(pa, lower_bound, norm_g.reshape(1, w))


def _attn_kernel(sink_ref, q_ref, qs_ref, k_ref, v_ref, o_ref, kz_s, *, lc):
    tt = k_ref.shape[0]
    blk = ATTN_BLOCK
    hd = B_HEAD_DIM
    j = pl.program_id(1)
    n_ctx_blk = lc // blk
    n_blk = tt // blk
    n_sub = q_ref.shape[0] // blk
    gw = B_GROUP * blk
    nt = (((1,), (1,)), ((), ()))
    tn = (((0,), (0,)), ((), ()))

    @pl.when(j == 0)
    def _():
        lane = lax.broadcasted_iota(jnp.int32, (1, B_KV_WIDTH), 1)
        for g in range(B_KV_HEADS):
            keep = (lane // hd) == g

            def body(i, carry):
                rows = pl.ds(pl.multiple_of(i * blk, blk), blk)
                k = k_ref[rows, :]
                kz_s[g, rows, :] = jnp.where(keep, k, jnp.zeros_like(k))
                return carry
            lax.fori_loop(0, n_blk, body, 0)

    def q_stack(g, qi):
        parts = []
        for hh in range(B_GROUP):
            h = g * B_GROUP + hh
            src = q_ref if (h % 2) == g else qs_ref
            parts.append(src[qi * blk:(qi + 1) * blk, (h // 2) * 128:(h // 2 + 1) * 128])
        return jnp.concatenate(parts, axis=0)

    def scores(g, qi, rows):
        return lax.dot_general(kz_s[g, rows, :], q_stack(g, qi), nt, preferred_element_type=F32)

    def softmax_values(g, pieces):
        sink = jnp.concatenate([jnp.full((1, blk), sink_ref[g * B_GROUP + hh] * LOG2E, F32)
                                for hh in range(B_GROUP)], axis=1)
        m = sink
        for s, _ in pieces:
            m = jnp.maximum(m, jnp.max(s, axis=0, keepdims=True))
        den = jnp.exp2(sink - m)
        probs = []
        for s, _ in pieces:
            p = jnp.exp2(s - m)
            den = den + jnp.sum(p, axis=0, keepdims=True)
            probs.append(p.astype(BF16))
        values = jnp.concatenate([vv for _, vv in pieces], axis=0)
        acc = lax.dot_general(values, jnp.concatenate(probs, axis=0), tn,
                              preferred_element_type=F32)
        return acc[g * hd:(g + 1) * hd, :] * (1.0 / den)

    def write(qi, outs):
        for pair in range(B_Q_HEADS // 2):
            g, hh = (2 * pair) // B_GROUP, (2 * pair) % B_GROUP
            both = jnp.concatenate([outs[g][:, hh * blk:(hh + 1) * blk],
                                    outs[g][:, (hh + 1) * blk:(hh + 2) * blk]], axis=0)
            o_ref[qi * blk:(qi + 1) * blk, pair * 128:(pair + 1) * 128] = both.T.astype(o_ref.dtype)

    def ctx_scores(qi):
        return [[scores(g, qi, pl.ds(0, lc))] for g in range(B_KV_HEADS)]

    def ctx_pieces(qi, sc):
        return [[(sc[g][0], v_ref[0:lc, :])] for g in range(B_KV_HEADS)]

    def band_rows(jb):
        prev = pl.ds(pl.multiple_of((jb - 1) * blk, blk), blk)
        own = pl.ds(pl.multiple_of(jb * blk, blk), blk)
        nxt = pl.ds(pl.multiple_of(jnp.minimum(jb + 1, n_blk - 1) * blk, blk), blk)
        return prev, own, nxt, pl.ds(0, lc)

    def band_scores(qi, jb):
        return [[scores(g, qi, r) for r in band_rows(jb)] for g in range(B_KV_HEADS)]

    def band_pieces(jb, sc):
        key = lax.broadcasted_iota(jnp.int32, (blk, gw), 0)
        qry = lax.broadcasted_iota(jnp.int32, (blk, gw), 1) % blk
        keep_prev = key >= qry + jnp.where(jb > n_ctx_blk, 0, blk)
        keep_next = key <= qry - jnp.where(jb < n_blk - 1, 0, blk)
        masked = MASK_VALUE * LOG2E
        r_prev, r_own, r_next, r_ctx = band_rows(jb)
        out = []
        for g in range(B_KV_HEADS):
            sp, so, sn, sx = sc[g]
            out.append([(jnp.where(keep_prev, sp, masked), v_ref[r_prev, :]), (so, v_ref[r_own, :]),
                        (jnp.where(keep_next, sn, masked), v_ref[r_next, :]), (sx, v_ref[r_ctx, :])])
        return out

    first = j * n_sub

    @pl.when(first < n_ctx_blk)
    def _():
        sc = [ctx_scores(qi) for qi in range(n_sub)]
        for qi in range(n_sub):
            pieces = ctx_pieces(qi, sc[qi])
            write(qi, [softmax_values(g, pieces[g]) for g in range(B_KV_HEADS)])

    @pl.when(first >= n_ctx_blk)
    def _():
        sc = [band_scores(qi, first + qi) for qi in range(n_sub)]
        for qi in range(n_sub):
            pieces = band_pieces(first + qi, sc[qi])
            write(qi, [softmax_values(g, pieces[g]) for g in range(B_KV_HEADS)])


def _attn(pq, pqs, pk, pv, sink, nb, lc):
    tt = pq.shape[0]
    blk = ATTN_Q_BLOCKS * ATTN_BLOCK
    assert lc % blk == 0 and tt % blk == 0
    return pl.pallas_call(
        functools.partial(_attn_kernel, lc=lc),
        grid=(nb, tt // blk),
        in_specs=[pl.BlockSpec(memory_space=pltpu.SMEM),
                  pl.BlockSpec((blk, B_WIDTH), lambda b, j: (j, b)),
                  pl.BlockSpec((blk, B_WIDTH), lambda b, j: (j, b)),
                  pl.BlockSpec((tt, B_KV_WIDTH), lambda b, j: (0, b)),
                  pl.BlockSpec((tt, B_KV_WIDTH), lambda b, j: (0, b))],
        out_specs=pl.BlockSpec((blk, B_WIDTH), lambda b, j: (j, b)),
        out_shape=jax.ShapeDtypeStruct((tt, nb * B_WIDTH), BF16),
        scratch_shapes=[pltpu.VMEM((B_KV_HEADS, tt, B_KV_WIDTH), BF16)],
        compiler_params=_cparams(2),
        name="window_gqa",
    )(sink, pq, pqs, pk, pv)


def _s5_tile_index(reverse, i, n_ctx, n_all):
    if not reverse:
        return i
    return jnp.where(i < n_ctx, n_ctx - 1 - i, n_all + n_ctx - 1 - i)


def _s5_kernel(u_ref, a_ref, wb_ref, wc_ref, y_ref, u_s, y_s, h_s, *, nb, reverse):
    i = pl.program_id(0)
    n = C_NSTATE
    steps = u_ref.shape[0]
    halves = C_WIDTH // 128
    group = S5_GROUP
    n_groups = steps // group

    @pl.when(i == 0)
    def _():
        h_s[...] = jnp.zeros(h_s.shape, F32)

    for b in range(nb):
        for c in range(halves):
            lanes = slice(b * C_WIDTH + c * 128, b * C_WIDTH + (c + 1) * 128)
            u_s[c, pl.ds(b, steps, stride=nb), :] = u_ref[:, lanes]
    a_re = jnp.broadcast_to(a_ref[0:1, :], (nb, n))
    a_im = jnp.broadcast_to(a_ref[1:2, :], (nb, n))

    def drive(g):
        rows = slice(g * group * nb, (g + 1) * group * nb)
        u = jnp.concatenate([u_s[c, rows, :] for c in range(halves)], axis=1)
        return jnp.dot(u.astype(BF16), wb_ref[...], preferred_element_type=F32)

    order = list(range(n_groups))[::-1] if reverse else list(range(n_groups))
    lookahead = 2
    driven = {g: drive(g) for g in order[:lookahead]}
    h_re, h_im = h_s[:, 0:n], h_s[:, n:2 * n]
    for pos, g in enumerate(order):
        bu = driven.pop(g)
        states = [None] * group
        for t in (range(group - 1, -1, -1) if reverse else range(group)):
            slab = bu[t * nb:(t + 1) * nb, :]
            h_re, h_im = (a_re * h_re - a_im * h_im + slab[:, 0:n],
                          a_re * h_im + a_im * h_re + slab[:, n:2 * n])
            states[t] = jnp.concatenate([h_re, h_im], axis=1)
        if pos + lookahead < n_groups:
            nxt = order[pos + lookahead]
            driven[nxt] = drive(nxt)
        hb = jnp.concatenate(states, axis=0).astype(BF16)
        y = jnp.dot(hb, wc_ref[...], preferred_element_type=F32)
        rows = slice(g * group * nb, (g + 1) * group * nb)
        for c in range(halves):
            y_s[c, rows, :] = y[:, c * 128:(c + 1) * 128]
    h_s[:, 0:n] = h_re
    h_s[:, n:2 * n] = h_im
    for b in range(nb):
        for c in range(halves):
            lanes = slice(b * C_WIDTH + c * 128, b * C_WIDTH + (c + 1) * 128)
            y_ref[:, lanes] = y_s[c, pl.ds(b, steps, stride=nb), :]


def _s5_scan(pu, a, wb, wc, index, nb, lc, reverse):
    tt = pu.shape[0]
    ts = S5_TILE
    n_all = tt // ts
    n_ctx = lc // ts
    tile = lambda i: (_s5_tile_index(reverse, i, n_ctx, n_all), 0)
    split = lambda: pltpu.VMEM((C_WIDTH // 128, ts * nb, 128), F32)
    const = lambda stack: pl.BlockSpec((None, None) + stack.shape[2:], lambda i: tuple(index) + (0, 0))
    return pl.pallas_call(
        functools.partial(_s5_kernel, nb=nb, reverse=reverse),
        grid=(n_all,),
        in_specs=[pl.BlockSpec((ts, nb * C_WIDTH), tile), const(a), const(wb), const(wc)],
        out_specs=pl.BlockSpec((ts, nb * C_WIDTH), tile),
        out_shape=jax.ShapeDtypeStruct((tt, nb * C_WIDTH), F32),
        scratch_shapes=[split(), split(), pltpu.VMEM((nb, 2 * C_NSTATE), F32)],
        compiler_params=_cparams(1, V7X_VMEM_LIMIT),
        name="s5_scan",
    )(pu, a, wb, wc)


def _s5_params(a_re, a_im, log_dt, b_re, b_im, c_re, c_im):
    eye = jnp.eye(C_GROUPS, dtype=F32)
    dt = jnp.exp(log_dt)[..., None]
    mag = jnp.exp(a_re * dt)
    ang = a_im * dt
    abar_re, abar_im = mag * jnp.cos(ang), mag * jnp.sin(ang)
    den = a_re * a_re + a_im * a_im
    coef_re = ((abar_re - 1.0) * a_re + abar_im * a_im) / den
    coef_im = (abar_im * a_re - (abar_re - 1.0) * a_im) / den
    b_re, b_im = b_re[:, None], b_im[:, None]
    bbar_re = coef_re[..., None] * b_re - coef_im[..., None] * b_im
    bbar_im = coef_re[..., None] * b_im + coef_im[..., None] * b_re
    lead = a_re.shape[:2]
    a = jnp.stack([abar_re.reshape(lead + (C_NSTATE,)), abar_im.reshape(lead + (C_NSTATE,))], axis=2)
    drive = lambda bb: jnp.einsum('lkgpc,gh->lkgchp', bb, eye).reshape(lead + (C_WIDTH, C_NSTATE))
    read = lambda cc: jnp.einsum('lkgcp,gh->lkgphc', cc, eye).reshape(lead + (C_NSTATE, C_WIDTH))
    wb = jnp.concatenate([drive(bbar_re), drive(bbar_im)], axis=-1)
    wc = jnp.concatenate([read(c_re), -read(c_im)], axis=-2)
    return a, wb.astype(BF16), wc.astype(BF16)


def _out_rows(x, is_ctx, gt_c, gt_b, a_ref, b_ref, u_ref, yf_ref, yb_ref, d_ref, gw_ref, gb_ref, wo_ref):
    y = d_ref[...] * u_ref[...] + yf_ref[...] + yb_ref[...]
    gelu = 0.5 * y * (1.0 + jnp.tanh(math.sqrt(2.0 / math.pi) * (y + 0.044715 * (y * y * y))))
    hg = jnp.dot(gelu.astype(BF16), gw_ref[...], preferred_element_type=F32) + gb_ref[...]
    c = hg[:, :C_WIDTH] * jax.nn.sigmoid(hg[:, C_WIDTH:])
    mix = (jnp.dot(a_ref[...], wo_ref[0:A_WIDTH, :], preferred_element_type=F32)
           + jnp.dot(b_ref[...], wo_ref[A_WIDTH:A_WIDTH + B_WIDTH, :], preferred_element_type=F32)
           + jnp.dot(c.astype(BF16), wo_ref[A_WIDTH + B_WIDTH:, :], preferred_element_type=F32))
    return x + jnp.where(is_ctx, gt_c[0], gt_b[0]) * mix


def _tail_kernel(x_ref, *refs, lc, final):
    x = x_ref[...]
    tm = x.shape[0]
    is_ctx = _is_ctx_row(tm, lc, lc // tm if final else 0)
    x = _out_rows(x, is_ctx, *refs[:11])
    y = _ffn_rows(x, is_ctx, *refs[11:20])
    if final:
        fg_ref, o_ref = refs[20:]
        o_ref[0] = y * lax.rsqrt(jnp.mean(y * y, axis=-1, keepdims=True) + RMS_EPS) * fg_ref[...]
    else:
        refs[20][...] = y


def _layer_tail(X, mod, bases, gains, layer, a_o, b_o, pu, y2, s5_d, glu_w_s, glu_b, w_out_s, w1s, w2s, nb, lc,
                final_g=None):
    tt = X.shape[0]
    d = D_MODEL
    final = final_g is not None
    tm = ROW_TILE if final else TOKEN_TILE
    first = lc // tm if final else 0
    const = lambda shape: pl.BlockSpec(shape, lambda i, b: (0,) * len(shape))
    tile = lambda w: pl.BlockSpec((tm, w), lambda i, b: (i + first, b))
    extra_specs, extra, alias = [], (), {0: 0}
    out_spec, out_shape = tile(d), (tt, nb * d)
    if final:
        extra_specs, extra, alias = [const((1, d))], (final_g.reshape(1, d),), {}
        out_spec, out_shape = pl.BlockSpec((1, tm, d), lambda i, b: (b, i, 0)), (nb, tt - lc, d)
    return pl.pallas_call(
        functools.partial(_tail_kernel, lc=lc, final=final),
        grid=(tt // tm - first, nb),
        in_specs=[tile(d)] + _mod_specs(nb, bases[1], (2,)) + [
                 tile(A_WIDTH), tile(B_WIDTH), tile(C_WIDTH), tile(C_WIDTH), tile(C_WIDTH),
                 const((1, C_WIDTH)), _layer_weight(glu_w_s, (layer,)), const((1, 2 * C_WIDTH)),
                 _layer_weight(w_out_s, (layer,))]
                 + _mod_specs(nb, bases[2], (0, 1, 2)) + [
                 _row_spec(gains, layer * 3 + 2), _layer_weight(w1s, (layer, 1)), _layer_weight(w2s, (layer, 1))]
                 + extra_specs,
        out_specs=out_spec,
        out_shape=jax.ShapeDtypeStruct(out_shape, F32),
        input_output_aliases=alias,
        compiler_params=_cparams(2, V7X_VMEM_LIMIT),
        name="layer_tail",
    )(X, mod, mod, a_o, b_o, pu, y2[0], y2[1], s5_d.reshape(1, C_WIDTH), glu_w_s, glu_b.reshape(1, 2 * C_WIDTH),
      w_out_s, *([mod] * 6), gains, w1s, w2s, *extra)


def _rope_tables(length, lc):
    t = jnp.arange(length)
    pos = jnp.stack([(t // GRID_W).astype(F32), (t % GRID_W).astype(F32)], axis=1)
    inv_freq = ROPE_BASE ** (-jnp.arange(ROPE_PAIRS, dtype=F32) / ROPE_PAIRS)
    ang = pos[:, :, None] * inv_freq
    cos = jnp.cos(ang)[:, :, None, :]
    sin = jnp.sin(ang)[:, :, None, :]
    cos = jnp.broadcast_to(cos, (length, 2, 2, ROPE_PAIRS)).reshape(length, B_HEAD_DIM)
    sin = jnp.concatenate([-sin, sin], axis=2).reshape(length, B_HEAD_DIM)
    cos = jnp.concatenate([jnp.ones((lc, B_HEAD_DIM), F32), cos], axis=0)
    sin = jnp.concatenate([jnp.zeros((lc, B_HEAD_DIM), F32), sin], axis=0)
    return jnp.tile(cos, (1, 128 // B_HEAD_DIM)), jnp.tile(sin, (1, 128 // B_HEAD_DIM))


def kernel(x, c, ctx, c_ctx, ada_w, ada_b, norm_g, ffn_w1, ffn_w2, w_in, w_out, hgrn_lower_bounds,
           hgrn_norm_g, attn_sink, s5_a_re, s5_a_im, s5_log_dt, s5_b_re, s5_b_im, s5_c_re, s5_c_im,
           s5_d, s5_glu_w, s5_glu_b, final_norm_g):
    nb, length, d = x.shape
    lc = ctx.shape[1]
    tt = lc + length
    depth = ada_w.shape[0]
    assert d == D_MODEL and nb == 8 and lc % ROW_TILE == 0 and length % ROW_TILE == 0 and tt % TOKEN_TILE == 0

    act = jnp.concatenate([c, c_ctx[None], jnp.zeros((16 - nb - 1, d), F32)], axis=0)
    mod = _modulation(act, ada_w, ada_b).reshape(depth, 16, 3, 3, d)[:, :nb + 1]
    mod = mod.transpose(0, 2, 1, 3, 4).reshape(depth * 3 * (nb + 1) * 3, 1, d)
    cos_t, sin_t = _rope_tables(length, lc)
    lb_soft = jax.nn.softmax(hgrn_lower_bounds.astype(F32), axis=0)
    lower_bound = jnp.cumsum(lb_soft, axis=0) - lb_soft[0]

    w1s, w2s = ffn_w1.astype(BF16), ffn_w2.astype(BF16)
    w_in_s, w_out_s, glu_w_s = w_in.astype(BF16), w_out.astype(BF16), s5_glu_w.astype(BF16)
    s5a, s5wb, s5wc = _s5_params(s5_a_re, s5_a_im, s5_log_dt, s5_b_re, s5_b_im, s5_c_re, s5_c_im)
    gains = norm_g.reshape(depth * 3, 1, d)
    X = (ctx, x)
    for l in range(depth):
        base = [(l * 3 + sub) * (nb + 1) * 3 for sub in range(3)]
        X, pa, pq, pqs, pk, pv, pu = _layer_head(X, mod, base, gains, l, w1s, w2s, w_in_s, cos_t, sin_t, nb, lc)
        a_o = _hgrn(pa, lower_bound[l], hgrn_norm_g[l], nb, lc)
        b_o = _attn(pq, pqs, pk, pv, attn_sink[l], nb, lc)
        y2 = [_s5_scan(pu, s5a, s5wb, s5wc, (l, k), nb, lc, k == 1) for k in range(2)]
        X = _layer_tail(X, mod, base, gains, l, a_o, b_o, pu, y2, s5_d[l], glu_w_s, s5_glu_b[l], w_out_s, w1s, w2s,
                        nb, lc, final_g=final_norm_g if l == depth - 1 else None)
    return X
```

```python
import functools
import math

import jax
import jax.numpy as jnp
from jax import lax
from jax.experimental import pallas as pl
from jax.experimental.pallas import tpu as pltpu

F32 = jnp.float32
BF16 = jnp.bfloat16

D_MODEL = 1024
GRID_W = 64
RMS_EPS = 1e-6
D_FF = 2816
A_HEADS = 4
A_HEAD_DIM = 64
A_WIDTH = A_HEADS * A_HEAD_DIM
B_Q_HEADS = 8
B_KV_HEADS = 2
B_GROUP = B_Q_HEADS // B_KV_HEADS
B_HEAD_DIM = 64
B_WIDTH = B_Q_HEADS * B_HEAD_DIM
B_KV_WIDTH = B_KV_HEADS * B_HEAD_DIM
WINDOW = 128
ATTN_BLOCK = 128
ATTN_Q_BLOCKS = 2
ROPE_BASE = 10000.0
ROPE_PAIRS = B_HEAD_DIM // 4
MASK_VALUE = -1e9
LOG2E = math.log2(math.e)
C_GROUPS = 16
C_GROUP_CH = 16
C_WIDTH = C_GROUPS * C_GROUP_CH
C_STATE = 64
C_NSTATE = C_GROUPS * C_STATE
A_IN = 5 * A_WIDTH
D_IN = A_IN + B_WIDTH + 2 * B_KV_WIDTH + C_WIDTH

V7X_LANES = 128
V7X_SUBLANES = 8
V7X_VMEM_LIMIT = 56 * 1024 * 1024
MOD_ROWS = 2 * V7X_SUBLANES
MOD_COLS = 1024
ROW_TILE = 256
TOKEN_TILE = 576
S5_TILE = 256
S5_GROUP = 16
HGRN_CHUNK = 128
HGRN_SUB = V7X_SUBLANES
HGRN_PREP_ROWS = 128


def _cparams(n_axes, vmem=None):
    return pltpu.CompilerParams(dimension_semantics=("arbitrary",) * n_axes, vmem_limit_bytes=vmem)


def _split_hi_lo(x):
    hi = x.astype(BF16)
    lo = (x - hi.astype(F32)).astype(BF16)
    return hi, lo


def _mod_kernel(act_ref, w_ref, b_ref, o_ref):
    a = act_ref[...]
    a = (a * jax.nn.sigmoid(a)).astype(BF16)
    o_ref[0] = jnp.dot(a, w_ref[0].astype(BF16), preferred_element_type=F32) + b_ref[0]


def _modulation(act, ada_w, ada_b):
    depth, d, n = ada_w.shape
    tn = MOD_COLS
    rows = act.shape[0]
    return pl.pallas_call(
        _mod_kernel,
        grid=(depth, n // tn),
        in_specs=[pl.BlockSpec((rows, d), lambda l, j: (0, 0)),
                  pl.BlockSpec((1, d, tn), lambda l, j: (l, 0, j)),
                  pl.BlockSpec((1, 1, tn), lambda l, j: (l, 0, j))],
        out_specs=pl.BlockSpec((1, rows, tn), lambda l, j: (l, 0, j)),
        out_shape=jax.ShapeDtypeStruct((depth, rows, n), F32),
        compiler_params=_cparams(2),
        name="adaln_mod",
    )(act, ada_w, ada_b.reshape(depth, 1, n))


def _is_ctx_row(tm, lc, first_tile=0):
    return lax.broadcasted_iota(jnp.int32, (tm, D_MODEL), 0) < lc - (pl.program_id(0) + first_tile) * tm


def _adaln(x, g, is_ctx, sh_c, sh_b, sc_c, sc_b):
    r = lax.rsqrt(jnp.mean(x * x, axis=-1, keepdims=True) + RMS_EPS)
    gain = jnp.where(is_ctx, g * (1.0 + sc_c[0]), g * (1.0 + sc_b[0]))
    return (x * r) * gain + jnp.where(is_ctx, sh_c[0], sh_b[0])


def _mod_specs(nb, base, comps):
    specs = []
    for comp in comps:
        specs.append(pl.BlockSpec((1, 1, D_MODEL), lambda i, b, comp=comp: (base + nb * 3 + comp, 0, 0)))
        specs.append(pl.BlockSpec((1, 1, D_MODEL), lambda i, b, comp=comp: (base + b * 3 + comp, 0, 0)))
    return specs


def _row_spec(table, row):
    return pl.BlockSpec((1, 1, table.shape[-1]), lambda i, b: (row, 0, 0))


def _ffn_rows(x, is_ctx, sh_c, sh_b, sc_c, sc_b, gt_c, gt_b, g_ref, w1_ref, w2_ref):
    h = _adaln(x, g_ref[0], is_ctx, sh_c, sh_b, sc_c, sc_b).astype(BF16)
    gu = jnp.dot(h, w1_ref[...], preferred_element_type=F32)
    gate = gu[:, :D_FF]
    a = (gate * jax.nn.sigmoid(gate) * gu[:, D_FF:]).astype(BF16)
    y = jnp.dot(a, w2_ref[...], preferred_element_type=F32)
    return x + jnp.where(is_ctx, 0.5 * gt_c[0], 0.5 * gt_b[0]) * y


def _layer_weight(stack, index):
    lead = len(index)
    return pl.BlockSpec((None,) * lead + stack.shape[lead:], lambda i, b: tuple(index) + (0, 0),
                        pipeline_mode=pl.Buffered(1))


def _rope(x, cos, sin, first_half):
    w = x.shape[-1]
    nxt = pltpu.roll(x, w - ROPE_PAIRS, axis=1)
    prv = pltpu.roll(x, ROPE_PAIRS, axis=1)
    return x * cos + jnp.where(first_half, nxt, prv) * sin


def _proj_rows(x, is_ctx, sh_c, sh_b, sc_c, sc_b, g_ref, w_ref, cos_ref, sin_ref,
               pa_ref, pq_ref, pqs_ref, pk_ref, pv_ref, pu_ref):
    h = _adaln(x, g_ref[0], is_ctx, sh_c, sh_b, sc_c, sc_b).astype(BF16)
    p = jnp.dot(h, w_ref[...], preferred_element_type=F32)
    o = A_IN
    pa_ref[...] = p[:, :o]
    cos = cos_ref[...]
    sin = sin_ref[...]
    lane = lax.broadcasted_iota(jnp.int32, (1, B_WIDTH), 1)
    first_half = (lane % (2 * ROPE_PAIRS)) < ROPE_PAIRS
    cos_q = jnp.concatenate([cos] * (B_WIDTH // V7X_LANES), axis=1)
    sin_q = jnp.concatenate([sin] * (B_WIDTH // V7X_LANES), axis=1)
    q = _rope(p[:, o:o + B_WIDTH], cos_q, sin_q, first_half)
    q = q * (B_HEAD_DIM ** -0.5 * LOG2E)
    pq_ref[...] = q.astype(BF16)
    low_head = (lane % V7X_LANES) < B_HEAD_DIM
    q_sw = jnp.where(low_head, pltpu.roll(q, B_WIDTH - B_HEAD_DIM, axis=1), pltpu.roll(q, B_HEAD_DIM, axis=1))
    pqs_ref[...] = q_sw.astype(BF16)
    o += B_WIDTH
    k = _rope(p[:, o:o + B_KV_WIDTH], cos, sin, first_half[:, :B_KV_WIDTH])
    pk_ref[...] = k.astype(BF16)
    o += B_KV_WIDTH
    pv_ref[...] = p[:, o:o + B_KV_WIDTH].astype(BF16)
    o += B_KV_WIDTH
    pu_ref[...] = p[:, o:o + C_WIDTH]


def _head_kernel(*refs, lc, entry):
    if entry:
        ctx_ref, lat_ref, *refs = refs
        x = jnp.where(pl.program_id(0) * ctx_ref.shape[1] < lc, ctx_ref[0], lat_ref[0])
    else:
        x_ref, *refs = refs
        x = x_ref[...]
    ffn_refs, proj_refs, (x_out, *proj_outs) = refs[:9], refs[9:17], refs[17:]
    is_ctx = _is_ctx_row(x.shape[0], lc)
    x = _ffn_rows(x, is_ctx, *ffn_refs)
    x_out[...] = x
    _proj_rows(x, is_ctx, *proj_refs, *proj_outs)


def _layer_head(X, mod, bases, gains, layer, w1s, w2s, w_in_s, cos_t, sin_t, nb, lc):
    d = D_MODEL
    entry = isinstance(X, tuple)
    tm = ROW_TILE if entry else TOKEN_TILE
    tile = lambda w: pl.BlockSpec((tm, w), lambda i, b: (i, b))
    if entry:
        ctx, lat = X
        nct = lc // tm
        tt = lc + lat.shape[1]
        srcs = [pl.BlockSpec((1, tm, d), lambda i, b: (b, jnp.minimum(i, nct - 1), 0)),
                pl.BlockSpec((1, tm, d), lambda i, b: (b, jnp.maximum(i - nct, 0), 0))]
        args, alias = (ctx, lat), {}
    else:
        tt = X.shape[0]
        srcs, args, alias = [tile(d)], (X,), {0: 0}
    widths = (d, A_IN, B_WIDTH, B_WIDTH, B_KV_WIDTH, B_KV_WIDTH, C_WIDTH)
    dtypes = (F32, F32, BF16, BF16, BF16, BF16, F32)
    table = pl.BlockSpec((tm, V7X_LANES), lambda i, b: (i, 0))
    return pl.pallas_call(
        functools.partial(_head_kernel, lc=lc, entry=entry),
        grid=(tt // tm, nb),
        in_specs=srcs + _mod_specs(nb, bases[0], (0, 1, 2)) + [
                 _row_spec(gains, layer * 3), _layer_weight(w1s, (layer, 0)), _layer_weight(w2s, (layer, 0))]
                 + _mod_specs(nb, bases[1], (0, 1)) + [
                 _row_spec(gains, layer * 3 + 1), _layer_weight(w_in_s, (layer,)), table, table],
        out_specs=[tile(w) for w in widths],
        out_shape=[jax.ShapeDtypeStruct((tt, nb * w), dt) for w, dt in zip(widths, dtypes)],
        input_output_aliases=alias,
        compiler_params=_cparams(2, V7X_VMEM_LIMIT),
        name="layer_head",
    )(*args, *([mod] * 6), gains, w1s, w2s, *([mod] * 4), gains, w_in_s, cos_t, sin_t)


def _hgrn_pairs(lo, hi, reverse):
    if hi - lo <= HGRN_SUB:
        return []
    mid = (lo + hi) // 2
    here = (lo, mid, mid - lo, mid) if reverse else (mid, lo, mid - lo, mid - 1)
    return [here] + _hgrn_pairs(lo, mid, reverse) + _hgrn_pairs(mid, hi, reverse)


def _hgrn_kernel(pa_ref, lb_ref, ng_ref, o_ref,
                 qs_s, kf_s, kb_s, ks_s, bf_s, bb_s, v_s, o_s, p_s, st_s, mask_s, ones_s, tri_s, *, lc):
    tt = pa_ref.shape[0]
    w = A_WIDTH
    ch = HGRN_CHUNK
    sub = HGRN_SUB
    n_sub = ch // sub
    blk = HGRN_PREP_ROWS
    n_blk = tt // blk
    n_chunks = tt // ch
    n_ctx_chunks = lc // ch
    nt = (((1,), (1,)), ((), ()))
    tn = (((0,), (0,)), ((), ()))

    row = lax.broadcasted_iota(jnp.int32, (w, w), 0)
    col = lax.broadcasted_iota(jnp.int32, (w, w), 1)
    same_head = jnp.where((row // A_HEAD_DIM) == (col // A_HEAD_DIM), 1.0, 0.0)
    mask_s[...] = same_head
    ones_s[...] = same_head.astype(BF16)
    lane = lax.broadcasted_iota(jnp.int32, (1, w), 1)
    head_mask = [jnp.where(lane // A_HEAD_DIM == h, 1.0, 0.0) for h in range(A_HEADS)]
    r = lax.broadcasted_iota(jnp.int32, (blk, blk), 0)
    c = lax.broadcasted_iota(jnp.int32, (blk, blk), 1)
    same_chunk = (r // ch) == (c // ch)
    tri_s[0] = jnp.where(same_chunk, jnp.where(c <= r, 1.0, 0.0), 0.0).astype(BF16)
    tri_s[1] = jnp.where(same_chunk, jnp.where(c >= r, 1.0, 0.0), 0.0).astype(BF16)
    st_s[...] = jnp.zeros(st_s.shape, F32)

    def chunk_cumsum(logf, tri):
        hi = logf.astype(BF16)
        rem = logf - hi.astype(F32)
        mid = rem.astype(BF16)
        lo = (rem - mid.astype(F32)).astype(BF16)
        return (jnp.dot(tri, hi, preferred_element_type=F32)
                + jnp.dot(tri, mid, preferred_element_type=F32)
                + jnp.dot(tri, lo, preferred_element_type=F32))

    def load(ref, rows):
        return jnp.concatenate([ref[c, rows, :] for c in range(w // V7X_LANES)], axis=1)

    def store(ref, rows, val):
        for c in range(w // V7X_LANES):
            ref[c, rows, :] = val[:, c * V7X_LANES:(c + 1) * V7X_LANES]

    def load_row(ref, r):
        return load(ref, pl.ds(r, sub, stride=0))

    def prep_body(i, carry):
        rows = pl.ds(pl.multiple_of(i * blk, blk), blk)
        q = pa_ref[rows, 0:w]
        qs_s[rows, :] = q * jax.nn.sigmoid(q)
        store(v_s, rows, pa_ref[rows, w:2 * w])
        o_s[rows, :] = jnp.zeros((blk, w), F32)
        k_sum = None
        for d, (k_s, b_s) in enumerate(((kf_s, bf_s), (kb_s, bb_s))):
            lb = lb_ref[d:d + 1, :]
            z = pa_ref[rows, (2 + d) * w:(3 + d) * w]
            t = jnp.exp(-jnp.abs(z))
            big = 1.0 / (1.0 + t)
            small = t * big
            f = lb + (1.0 - lb) * jnp.where(z >= 0, big, small)
            k = (1.0 - lb) * jnp.where(z >= 0, small, big)
            store(k_s, rows, k)
            k_sum = k if k_sum is None else k_sum + k
            store(b_s, rows, chunk_cumsum(jnp.log2(f), tri_s[d]))
        store(ks_s, rows, k_sum)
        return carry
    lax.fori_loop(0, n_blk, prep_body, 0, unroll=3)

    def expand(blocks):
        return jnp.concatenate([x * head_mask[h] for h in range(A_HEADS) for x in blocks], axis=0).astype(BF16)

    def expand_cached(masked, s0, n):
        j0 = s0 // sub
        return jnp.concatenate([masked[j][h] for h in range(A_HEADS) for j in range(j0, j0 + n // sub)],
                               axis=0).astype(BF16)

    def diag_scores(r0):
        t_i = lax.broadcasted_iota(jnp.int32, (sub, w), 0)
        for j in range(n_sub):
            base = r0 + j * sub
            rows = pl.ds(base, sub)
            q = qs_s[rows, :]
            bf = load(bf_s, rows)
            bb = load(bb_s, rows)
            for s in range(sub):
                kf_row = load_row(kf_s, base + s)
                kb_row = load_row(kb_s, base + s)
                arg = jnp.where(t_i >= s, bf - load_row(bf_s, base + s), bb - load_row(bb_s, base + s))
                kk = jnp.where(t_i > s, kf_row, jnp.where(t_i < s, kb_row, load_row(ks_s, base + s)))
                p_s[(j * sub + s) * sub:(j * sub + s + 1) * sub, :] = q * jnp.exp2(arg) * kk
        return jnp.dot(p_s[...].astype(BF16), ones_s[...], preferred_element_type=F32)

    def diag_apply(r0, a):
        out = []
        for j in range(n_sub):
            acc = None
            for s in range(sub):
                v_row = load_row(v_s, r0 + j * sub + s)
                term = a[(j * sub + s) * sub:(j * sub + s + 1) * sub, :] * v_row
                acc = term if acc is None else acc + term
            out.append(acc)
        return out

    def dir_scores(r0, d):
        k_s, b_s = (kf_s, bf_s) if d == 0 else (kb_s, bb_s)
        rows8 = lambda off: pl.ds(r0 + off, sub)
        q_blk = lambda off: qs_s[rows8(off), :]
        b_blk = lambda off: load(b_s, rows8(off))
        k_blk = lambda off: load(k_s, rows8(off))
        offs = lambda start, n: range(start, start + n, sub)
        b_last = load_row(b_s, r0 + ((ch - 1) if d == 0 else 0))
        pairs = _hgrn_pairs(0, ch, d == 1)
        scores = []
        for t0, s0, n, ref in pairs:
            b_ref = load_row(b_s, r0 + ref)
            qx = jnp.concatenate([q_blk(o) * jnp.exp2(b_blk(o) - b_ref) for o in offs(t0, n)], axis=0).astype(BF16)
            kx = expand([k_blk(o) * jnp.exp2(b_ref - b_blk(o)) for o in offs(s0, n)])
            scores.append(lax.dot_general(qx, kx, nt, preferred_element_type=F32))
        st = st_s[d]
        qe = jnp.concatenate([q_blk(o) * jnp.exp2(b_blk(o)) for o in offs(0, ch)], axis=0).astype(BF16)
        kd = jnp.concatenate([k_blk(o) * jnp.exp2(b_last - b_blk(o)) for o in offs(0, ch)], axis=0).astype(BF16)
        o = lax.dot_general(qe, st.astype(BF16), nt, preferred_element_type=F32)
        ut = lax.dot_general(load(v_s, pl.ds(r0, ch)).astype(BF16), kd, tn, preferred_element_type=F32)
        return pairs, scores, o, ut, st, b_last[0:1, :]

    def masked_values(r0):
        blocks = [load(v_s, pl.ds(r0 + j * sub, sub)) for j in range(n_sub)]
        return [[v * head_mask[h] for h in range(A_HEADS)] for v in blocks]

    def dir_values(pairs, scores, v_masked):
        return [jnp.dot(sc.astype(BF16), expand_cached(v_masked, s0, n), preferred_element_type=F32)
                for (t0, s0, n, ref), sc in zip(pairs, scores)]

    def dir_finish(d, pairs, contribs, o, ut, st, b_last):
        st_s[d] = st * jnp.exp2(b_last) + ut * mask_s[...]
        out = [o[j * sub:(j + 1) * sub, :] for j in range(n_sub)]
        for (t0, s0, n, ref), contrib in zip(pairs, contribs):
            for jj in range(n // sub):
                out[t0 // sub + jj] = out[t0 // sub + jj] + contrib[jj * sub:(jj + 1) * sub, :]
        return out

    def chunk_body(i, carry):
        cb = jnp.where(i < n_ctx_chunks, n_ctx_chunks - 1 - i, n_chunks + n_ctx_chunks - 1 - i)
        rf = pl.multiple_of(i * ch, ch)
        rb = pl.multiple_of(cb * ch, ch)
        pf, scf, of, utf, stf, blf = dir_scores(rf, 0)
        pb, scb, ob, utb, stb, blb = dir_scores(rb, 1)
        a = diag_scores(rf)
        cf = dir_values(pf, scf, masked_values(rf))
        cbw = dir_values(pb, scb, masked_values(rb))
        o_fwd = dir_finish(0, pf, cf, of, utf, stf, blf)
        o_bwd = dir_finish(1, pb, cbw, ob, utb, stb, blb)
        o_diag = diag_apply(rf, a)
        rows_f = pl.ds(rf, ch)
        o_s[rows_f, :] = o_s[rows_f, :] + jnp.concatenate([x + y for x, y in zip(o_diag, o_fwd)], axis=0)
        rows_b = pl.ds(rb, ch)
        o_s[rows_b, :] = o_s[rows_b, :] + jnp.concatenate(o_bwd, axis=0)
        return carry
    lax.fori_loop(0, n_chunks, chunk_body, 0, unroll=2)

    def out_body(i, carry):
        rows = pl.ds(pl.multiple_of(i * blk, blk), blk)
        o = o_s[rows, :]
        hi, lo = _split_hi_lo(o * o)
        ones = ones_s[...]
        ms = (jnp.dot(hi, ones, preferred_element_type=F32)
              + jnp.dot(lo, ones, preferred_element_type=F32)) * (1.0 / A_HEAD_DIM)
        g = pa_ref[rows, 4 * w:5 * w]
        o_ref[rows, :] = (o * lax.rsqrt(ms + RMS_EPS) * ng_ref[...] * (g * jax.nn.sigmoid(g))).astype(o_ref.dtype)
        return carry
    lax.fori_loop(0, n_blk, out_body, 0, unroll=3)


def _hgrn(pa, lower_bound, norm_g, nb, lc):
    tt = pa.shape[0]
    w = A_WIDTH
    seq = lambda: pltpu.VMEM((tt, w), F32)
    split = lambda: pltpu.VMEM((w // V7X_LANES, tt, V7X_LANES), F32)
    return pl.pallas_call(
        functools.partial(_hgrn_kernel, lc=lc),
        grid=(nb,),
        in_specs=[pl.BlockSpec((tt, A_IN), lambda b: (0, b)),
                  pl.BlockSpec((2, w), lambda b: (0, 0)),
                  pl.BlockSpec((1, w), lambda b: (0, 0))],
        out_specs=pl.BlockSpec((tt, w), lambda b: (0, b)),
        out_shape=jax.ShapeDtypeStruct((tt, nb * w), BF16),
        scratch_shapes=[seq(), split(), split(), split(), split(), split(), split(), seq(),
                        pltpu.VMEM((HGRN_CHUNK * HGRN_SUB, w), F32),
                        pltpu.VMEM((2, w, w), F32), pltpu.VMEM((w, w), F32), pltpu.VMEM((w, w), BF16),
                        pltpu.VMEM((2, HGRN_PREP_ROWS, HGRN_PREP_ROWS), BF16)],
        compiler_params=_cparams(1, V7X_VMEM_LIMIT),
        name="hgrn2_mixer",
    )(pa, lower_bound, norm_g.reshape(1, w))


def _attn_kernel(sink_ref, q_ref, qs_ref, k_ref, v_ref, o_ref, kz_s, *, lc):
    tt = k_ref.shape[0]
    blk = ATTN_BLOCK
    hd = B_HEAD_DIM
    j = pl.program_id(1)
    n_ctx_blk = lc // blk
    n_blk = tt // blk
    n_sub = q_ref.shape[0] // blk
    gw = B_GROUP * blk
    nt = (((1,), (1,)), ((), ()))
    tn = (((0,), (0,)), ((), ()))

    @pl.when(j == 0)
    def _():
        lane = lax.broadcasted_iota(jnp.int32, (1, B_KV_WIDTH), 1)
        for g in range(B_KV_HEADS):
            keep = (lane // hd) == g

            def body(i, carry):
                rows = pl.ds(pl.multiple_of(i * blk, blk), blk)
                k = k_ref[rows, :]
                kz_s[g, rows, :] = jnp.where(keep, k, jnp.zeros_like(k))
                return carry
            lax.fori_loop(0, n_blk, body, 0)

    def q_stack(g, qi):
        parts = []
        for hh in range(B_GROUP):
            h = g * B_GROUP + hh
            src = q_ref if (h % 2) == g else qs_ref
            parts.append(src[qi * blk:(qi + 1) * blk, (h // 2) * V7X_LANES:(h // 2 + 1) * V7X_LANES])
        return jnp.concatenate(parts, axis=0)

    def scores(g, qi, rows):
        return lax.dot_general(kz_s[g, rows, :], q_stack(g, qi), nt, preferred_element_type=F32)

    def softmax_values(g, pieces):
        sink = jnp.concatenate([jnp.full((1, blk), sink_ref[g * B_GROUP + hh] * LOG2E, F32)
                                for hh in range(B_GROUP)], axis=1)
        m = sink
        for s, _ in pieces:
            m = jnp.maximum(m, jnp.max(s, axis=0, keepdims=True))
        den = jnp.exp2(sink - m)
        probs = []
        for s, _ in pieces:
            p = jnp.exp2(s - m)
            den = den + jnp.sum(p, axis=0, keepdims=True)
            probs.append(p.astype(BF16))
        values = jnp.concatenate([vv for _, vv in pieces], axis=0)
        acc = lax.dot_general(values, jnp.concatenate(probs, axis=0), tn,
                              preferred_element_type=F32)
        return acc[g * hd:(g + 1) * hd, :] * (1.0 / den)

    def write(qi, outs):
        for pair in range(B_Q_HEADS // 2):
            g, hh = (2 * pair) // B_GROUP, (2 * pair) % B_GROUP
            both = jnp.concatenate([outs[g][:, hh * blk:(hh + 1) * blk],
                                    outs[g][:, (hh + 1) * blk:(hh + 2) * blk]], axis=0)
            o_ref[qi * blk:(qi + 1) * blk, pair * V7X_LANES:(pair + 1) * V7X_LANES] = both.T.astype(o_ref.dtype)

    def ctx_scores(qi):
        return [[scores(g, qi, pl.ds(0, lc))] for g in range(B_KV_HEADS)]

    def ctx_pieces(qi, sc):
        return [[(sc[g][0], v_ref[0:lc, :])] for g in range(B_KV_HEADS)]

    def band_rows(jb):
        prev = pl.ds(pl.multiple_of((jb - 1) * blk, blk), blk)
        own = pl.ds(pl.multiple_of(jb * blk, blk), blk)
        nxt = pl.ds(pl.multiple_of(jnp.minimum(jb + 1, n_blk - 1) * blk, blk), blk)
        return prev, own, nxt, pl.ds(0, lc)

    def band_scores(qi, jb):
        return [[scores(g, qi, r) for r in band_rows(jb)] for g in range(B_KV_HEADS)]

    def band_pieces(jb, sc):
        key = lax.broadcasted_iota(jnp.int32, (blk, gw), 0)
        qry = lax.broadcasted_iota(jnp.int32, (blk, gw), 1) % blk
        keep_prev = key >= qry + jnp.where(jb > n_ctx_blk, 0, blk)
        keep_next = key <= qry - jnp.where(jb < n_blk - 1, 0, blk)
        masked = MASK_VALUE * LOG2E
        r_prev, r_own, r_next, r_ctx = band_rows(jb)
        out = []
        for g in range(B_KV_HEADS):
            sp, so, sn, sx = sc[g]
            out.append([(jnp.where(keep_prev, sp, masked), v_ref[r_prev, :]), (so, v_ref[r_own, :]),
                        (jnp.where(keep_next, sn, masked), v_ref[r_next, :]), (sx, v_ref[r_ctx, :])])
        return out

    first = j * n_sub

    @pl.when(first < n_ctx_blk)
    def _():
        sc = [ctx_scores(qi) for qi in range(n_sub)]
        for qi in range(n_sub):
            pieces = ctx_pieces(qi, sc[qi])
            write(qi, [softmax_values(g, pieces[g]) for g in range(B_KV_HEADS)])

    @pl.when(first >= n_ctx_blk)
    def _():
        sc = [band_scores(qi, first + qi) for qi in range(n_sub)]
        for qi in range(n_sub):
            pieces = band_pieces(first + qi, sc[qi])
            write(qi, [softmax_values(g, pieces[g]) for g in range(B_KV_HEADS)])


def _attn(pq, pqs, pk, pv, sink, nb, lc):
    tt = pq.shape[0]
    blk = ATTN_Q_BLOCKS * ATTN_BLOCK
    assert WINDOW == ATTN_BLOCK and lc % blk == 0 and tt % blk == 0
    return pl.pallas_call(
        functools.partial(_attn_kernel, lc=lc),
        grid=(nb, tt // blk),
        in_specs=[pl.BlockSpec(memory_space=pltpu.SMEM),
                  pl.BlockSpec((blk, B_WIDTH), lambda b, j: (j, b)),
                  pl.BlockSpec((blk, B_WIDTH), lambda b, j: (j, b)),
                  pl.BlockSpec((tt, B_KV_WIDTH), lambda b, j: (0, b)),
                  pl.BlockSpec((tt, B_KV_WIDTH), lambda b, j: (0, b))],
        out_specs=pl.BlockSpec((blk, B_WIDTH), lambda b, j: (j, b)),
        out_shape=jax.ShapeDtypeStruct((tt, nb * B_WIDTH), BF16),
        scratch_shapes=[pltpu.VMEM((B_KV_HEADS, tt, B_KV_WIDTH), BF16)],
        compiler_params=_cparams(2),
        name="window_gqa",
    )(sink, pq, pqs, pk, pv)


def _s5_tile_index(reverse, i, n_ctx, n_all):
    if not reverse:
        return i
    return jnp.where(i < n_ctx, n_ctx - 1 - i, n_all + n_ctx - 1 - i)


def _s5_kernel(u_ref, a_ref, wb_ref, wc_ref, y_ref, u_s, y_s, h_s, *, nb, reverse):
    i = pl.program_id(0)
    n = C_NSTATE
    steps = u_ref.shape[0]
    halves = C_WIDTH // V7X_LANES
    group = S5_GROUP
    n_groups = steps // group

    @pl.when(i == 0)
    def _():
        h_s[...] = jnp.zeros(h_s.shape, F32)

    for b in range(nb):
        for c in range(halves):
            lanes = slice(b * C_WIDTH + c * V7X_LANES, b * C_WIDTH + (c + 1) * V7X_LANES)
            u_s[c, pl.ds(b, steps, stride=nb), :] = u_ref[:, lanes]
    a_re = jnp.broadcast_to(a_ref[0:1, :], (nb, n))
    a_im = jnp.broadcast_to(a_ref[1:2, :], (nb, n))

    def drive(g):
        rows = slice(g * group * nb, (g + 1) * group * nb)
        u = jnp.concatenate([u_s[c, rows, :] for c in range(halves)], axis=1)
        return jnp.dot(u.astype(BF16), wb_ref[...], preferred_element_type=F32)

    order = list(range(n_groups))[::-1] if reverse else list(range(n_groups))
    lookahead = 2
    driven = {g: drive(g) for g in order[:lookahead]}
    h_re, h_im = h_s[:, 0:n], h_s[:, n:2 * n]
    for pos, g in enumerate(order):
        bu = driven.pop(g)
        states = [None] * group
        for t in (range(group - 1, -1, -1) if reverse else range(group)):
            slab = bu[t * nb:(t + 1) * nb, :]
            h_re, h_im = (a_re * h_re - a_im * h_im + slab[:, 0:n],
                          a_re * h_im + a_im * h_re + slab[:, n:2 * n])
            states[t] = jnp.concatenate([h_re, h_im], axis=1)
        if pos + lookahead < n_groups:
            nxt = order[pos + lookahead]
            driven[nxt] = drive(nxt)
        hb = jnp.concatenate(states, axis=0).astype(BF16)
        y = jnp.dot(hb, wc_ref[...], preferred_element_type=F32)
        rows = slice(g * group * nb, (g + 1) * group * nb)
        for c in range(halves):
            y_s[c, rows, :] = y[:, c * V7X_LANES:(c + 1) * V7X_LANES]
    h_s[:, 0:n] = h_re
    h_s[:, n:2 * n] = h_im
    for b in range(nb):
        for c in range(halves):
            lanes = slice(b * C_WIDTH + c * V7X_LANES, b * C_WIDTH + (c + 1) * V7X_LANES)
            y_ref[:, lanes] = y_s[c, pl.ds(b, steps, stride=nb), :]


def _s5_scan(pu, a, wb, wc, index, nb, lc, reverse):
    tt = pu.shape[0]
    ts = S5_TILE
    n_all = tt // ts
    n_ctx = lc // ts
    tile = lambda i: (_s5_tile_index(reverse, i, n_ctx, n_all), 0)
    split = lambda: pltpu.VMEM((C_WIDTH // V7X_LANES, ts * nb, V7X_LANES), F32)
    const = lambda stack: pl.BlockSpec((None, None) + stack.shape[2:], lambda i: tuple(index) + (0, 0))
    return pl.pallas_call(
        functools.partial(_s5_kernel, nb=nb, reverse=reverse),
        grid=(n_all,),
        in_specs=[pl.BlockSpec((ts, nb * C_WIDTH), tile), const(a), const(wb), const(wc)],
        out_specs=pl.BlockSpec((ts, nb * C_WIDTH), tile),
        out_shape=jax.ShapeDtypeStruct((tt, nb * C_WIDTH), F32),
        scratch_shapes=[split(), split(), pltpu.VMEM((nb, 2 * C_NSTATE), F32)],
        compiler_params=_cparams(1, V7X_VMEM_LIMIT),
        name="s5_scan",
    )(pu, a, wb, wc)


def _s5_params(a_re, a_im, log_dt, b_re, b_im, c_re, c_im):
    eye = jnp.eye(C_GROUPS, dtype=F32)
    dt = jnp.exp(log_dt)[..., None]
    mag = jnp.exp(a_re * dt)
    ang = a_im * dt
    abar_re, abar_im = mag * jnp.cos(ang), mag * jnp.sin(ang)
    den = a_re * a_re + a_im * a_im
    coef_re = ((abar_re - 1.0) * a_re + abar_im * a_im) / den
    coef_im = (abar_im * a_re - (abar_re - 1.0) * a_im) / den
    b_re, b_im = b_re[:, None], b_im[:, None]
    bbar_re = coef_re[..., None] * b_re - coef_im[..., None] * b_im
    bbar_im = coef_re[..., None] * b_im + coef_im[..., None] * b_re
    lead = a_re.shape[:2]
    a = jnp.stack([abar_re.reshape(lead + (C_NSTATE,)), abar_im.reshape(lead + (C_NSTATE,))], axis=2)
    drive = lambda bb: jnp.einsum('lkgpc,gh->lkgchp', bb, eye).reshape(lead + (C_WIDTH, C_NSTATE))
    read = lambda cc: jnp.einsum('lkgcp,gh->lkgphc', cc, eye).reshape(lead + (C_NSTATE, C_WIDTH))
    wb = jnp.concatenate([drive(bbar_re), drive(bbar_im)], axis=-1)
    wc = jnp.concatenate([read(c_re), -read(c_im)], axis=-2)
    return a, wb.astype(BF16), wc.astype(BF16)


def _out_rows(x, is_ctx, gt_c, gt_b, a_ref, b_ref, u_ref, yf_ref, yb_ref, d_ref, gw_ref, gb_ref, wo_ref):
    y = d_ref[...] * u_ref[...] + yf_ref[...] + yb_ref[...]
    gelu = 0.5 * y * (1.0 + jnp.tanh(math.sqrt(2.0 / math.pi) * (y + 0.044715 * (y * y * y))))
    hg = jnp.dot(gelu.astype(BF16), gw_ref[...], preferred_element_type=F32) + gb_ref[...]
    c = hg[:, :C_WIDTH] * jax.nn.sigmoid(hg[:, C_WIDTH:])
    mix = (jnp.dot(a_ref[...], wo_ref[0:A_WIDTH, :], preferred_element_type=F32)
           + jnp.dot(b_ref[...], wo_ref[A_WIDTH:A_WIDTH + B_WIDTH, :], preferred_element_type=F32)
           + jnp.dot(c.astype(BF16), wo_ref[A_WIDTH + B_WIDTH:, :], preferred_element_type=F32))
    return x + jnp.where(is_ctx, gt_c[0], gt_b[0]) * mix


def _tail_kernel(x_ref, *refs, lc, final):
    x = x_ref[...]
    tm = x.shape[0]
    is_ctx = _is_ctx_row(tm, lc, lc // tm if final else 0)
    x = _out_rows(x, is_ctx, *refs[:11])
    y = _ffn_rows(x, is_ctx, *refs[11:20])
    if final:
        fg_ref, o_ref = refs[20:]
        o_ref[0] = y * lax.rsqrt(jnp.mean(y * y, axis=-1, keepdims=True) + RMS_EPS) * fg_ref[...]
    else:
        refs[20][...] = y


def _layer_tail(X, mod, bases, gains, layer, a_o, b_o, pu, y2, s5_d, glu_w_s, glu_b, w_out_s, w1s, w2s, nb, lc,
                final_g=None):
    tt = X.shape[0]
    d = D_MODEL
    final = final_g is not None
    tm = ROW_TILE if final else TOKEN_TILE
    first = lc // tm if final else 0
    const = lambda shape: pl.BlockSpec(shape, lambda i, b: (0,) * len(shape))
    tile = lambda w: pl.BlockSpec((tm, w), lambda i, b: (i + first, b))
    extra_specs, extra, alias = [], (), {0: 0}
    out_spec, out_shape = tile(d), (tt, nb * d)
    if final:
        extra_specs, extra, alias = [const((1, d))], (final_g.reshape(1, d),), {}
        out_spec, out_shape = pl.BlockSpec((1, tm, d), lambda i, b: (b, i, 0)), (nb, tt - lc, d)
    return pl.pallas_call(
        functools.partial(_tail_kernel, lc=lc, final=final),
        grid=(tt // tm - first, nb),
        in_specs=[tile(d)] + _mod_specs(nb, bases[1], (2,)) + [
                 tile(A_WIDTH), tile(B_WIDTH), tile(C_WIDTH), tile(C_WIDTH), tile(C_WIDTH),
                 const((1, C_WIDTH)), _layer_weight(glu_w_s, (layer,)), const((1, 2 * C_WIDTH)),
                 _layer_weight(w_out_s, (layer,))]
                 + _mod_specs(nb, bases[2], (0, 1, 2)) + [
                 _row_spec(gains, layer * 3 + 2), _layer_weight(w1s, (layer, 1)), _layer_weight(w2s, (layer, 1))]
                 + extra_specs,
        out_specs=out_spec,
        out_shape=jax.ShapeDtypeStruct(out_shape, F32),
        input_output_aliases=alias,
        compiler_params=_cparams(2, V7X_VMEM_LIMIT),
        name="layer_tail",
    )(X, mod, mod, a_o, b_o, pu, y2[0], y2[1], s5_d.reshape(1, C_WIDTH), glu_w_s, glu_b.reshape(1, 2 * C_WIDTH),
      w_out_s, *([mod] * 6), gains, w1s, w2s, *extra)


def _rope_tables(length, lc):
    t = jnp.arange(length)
    pos = jnp.stack([(t // GRID_W).astype(F32), (t % GRID_W).astype(F32)], axis=1)
    inv_freq = ROPE_BASE ** (-jnp.arange(ROPE_PAIRS, dtype=F32) / ROPE_PAIRS)
    ang = pos[:, :, None] * inv_freq
    cos = jnp.cos(ang)[:, :, None, :]
    sin = jnp.sin(ang)[:, :, None, :]
    cos = jnp.broadcast_to(cos, (length, 2, 2, ROPE_PAIRS)).reshape(length, B_HEAD_DIM)
    sin = jnp.concatenate([-sin, sin], axis=2).reshape(length, B_HEAD_DIM)
    cos = jnp.concatenate([jnp.ones((lc, B_HEAD_DIM), F32), cos], axis=0)
    sin = jnp.concatenate([jnp.zeros((lc, B_HEAD_DIM), F32), sin], axis=0)
    return jnp.tile(cos, (1, V7X_LANES // B_HEAD_DIM)), jnp.tile(sin, (1, V7X_LANES // B_HEAD_DIM))


def kernel(x, c, ctx, c_ctx, ada_w, ada_b, norm_g, ffn_w1, ffn_w2, w_in, w_out, hgrn_lower_bounds,
           hgrn_norm_g, attn_sink, s5_a_re, s5_a_im, s5_log_dt, s5_b_re, s5_b_im, s5_c_re, s5_c_im,
           s5_d, s5_glu_w, s5_glu_b, final_norm_g):
    nb, length, d = x.shape
    lc = ctx.shape[1]
    tt = lc + length
    depth = ada_w.shape[0]
    assert d == D_MODEL and nb == 8 and lc % ROW_TILE == 0 and length % ROW_TILE == 0 and tt % TOKEN_TILE == 0

    act = jnp.concatenate([c, c_ctx[None], jnp.zeros((MOD_ROWS - nb - 1, d), F32)], axis=0)
    mod = _modulation(act, ada_w, ada_b).reshape(depth, MOD_ROWS, 3, 3, d)[:, :nb + 1]
    mod = mod.transpose(0, 2, 1, 3, 4).reshape(depth * 3 * (nb + 1) * 3, 1, d)
    cos_t, sin_t = _rope_tables(length, lc)
    lb_soft = jax.nn.softmax(hgrn_lower_bounds.astype(F32), axis=0)
    lower_bound = jnp.cumsum(lb_soft, axis=0) - lb_soft[0]

    w1s, w2s = ffn_w1.astype(BF16), ffn_w2.astype(BF16)
    w_in_s, w_out_s, glu_w_s = w_in.astype(BF16), w_out.astype(BF16), s5_glu_w.astype(BF16)
    s5a, s5wb, s5wc = _s5_params(s5_a_re, s5_a_im, s5_log_dt, s5_b_re, s5_b_im, s5_c_re, s5_c_im)
    gains = norm_g.reshape(depth * 3, 1, d)
    X = (ctx, x)
    for l in range(depth):
        base = [(l * 3 + sub) * (nb + 1) * 3 for sub in range(3)]
        X, pa, pq, pqs, pk, pv, pu = _layer_head(X, mod, base, gains, l, w1s, w2s, w_in_s, cos_t, sin_t, nb, lc)
        a_o = _hgrn(pa, lower_bound[l], hgrn_norm_g[l], nb, lc)
        b_o = _attn(pq, pqs, pk, pv, attn_sink[l], nb, lc)
        y2 = [_s5_scan(pu, s5a, s5wb, s5wc, (l, k), nb, lc, k == 1) for k in range(2)]
        X = _layer_tail(X, mod, base, gains, l, a_o, b_o, pu, y2, s5_d[l], glu_w_s, s5_glu_b[l], w_out_s, w1s, w2s,
                        nb, lc, final_g=final_norm_g if l == depth - 1 else None)
    return X
```

```python
import functools
import math

import jax
import jax.numpy as jnp
from jax import lax
from jax.experimental import pallas as pl
from jax.experimental.pallas import tpu as pltpu

F32 = jnp.float32
BF16 = jnp.bfloat16

D_MODEL = 1024
GRID_W = 64
RMS_EPS = 1e-6
D_FF = 2816
A_HEADS = 4
A_HEAD_DIM = 64
A_WIDTH = A_HEADS * A_HEAD_DIM
B_Q_HEADS = 8
B_KV_HEADS = 2
B_GROUP = B_Q_HEADS // B_KV_HEADS
B_HEAD_DIM = 64
B_WIDTH = B_Q_HEADS * B_HEAD_DIM
B_KV_WIDTH = B_KV_HEADS * B_HEAD_DIM
WINDOW = 128
ATTN_BLOCK = 128
ATTN_Q_BLOCKS = 2
ROPE_BASE = 10000.0
ROPE_PAIRS = B_HEAD_DIM // 4
MASK_VALUE = -1e9
LOG2E = math.log2(math.e)
C_GROUPS = 16
C_GROUP_CH = 16
C_WIDTH = C_GROUPS * C_GROUP_CH
C_STATE = 64
C_NSTATE = C_GROUPS * C_STATE
A_IN = 5 * A_WIDTH
D_IN = A_IN + B_WIDTH + 2 * B_KV_WIDTH + C_WIDTH

V7X_LANES = 128
V7X_SUBLANES = 8
V7X_VMEM_LIMIT = 56 * 1024 * 1024
MOD_ROWS = 2 * V7X_SUBLANES
MOD_COLS = 1024
ROW_TILE = 256
TOKEN_TILE = 576
S5_TILE = 256
S5_GROUP = 16
HGRN_CHUNK = 128
HGRN_SUB = V7X_SUBLANES
HGRN_PREP_ROWS = 128


def _cparams(n_axes, vmem=None):
    return pltpu.CompilerParams(dimension_semantics=("arbitrary",) * n_axes, vmem_limit_bytes=vmem)


def _split_hi_lo(x):
    hi = x.astype(BF16)
    lo = (x - hi.astype(F32)).astype(BF16)
    return hi, lo


def _mod_kernel(act_ref, w_ref, b_ref, o_ref):
    a = act_ref[...]
    a = (a * jax.nn.sigmoid(a)).astype(BF16)
    o_ref[0] = jnp.dot(a, w_ref[0].astype(BF16), preferred_element_type=F32) + b_ref[0]


def _modulation(act, ada_w, ada_b):
    depth, d, n = ada_w.shape
    tn = MOD_COLS
    rows = act.shape[0]
    return pl.pallas_call(
        _mod_kernel,
        grid=(depth, n // tn),
        in_specs=[pl.BlockSpec((rows, d), lambda l, j: (0, 0)),
                  pl.BlockSpec((1, d, tn), lambda l, j: (l, 0, j)),
                  pl.BlockSpec((1, 1, tn), lambda l, j: (l, 0, j))],
        out_specs=pl.BlockSpec((1, rows, tn), lambda l, j: (l, 0, j)),
        out_shape=jax.ShapeDtypeStruct((depth, rows, n), F32),
        compiler_params=_cparams(2),
        name="adaln_mod",
    )(act, ada_w, ada_b.reshape(depth, 1, n))


def _is_ctx_row(tm, lc, first_tile=0):
    return lax.broadcasted_iota(jnp.int32, (tm, D_MODEL), 0) < lc - (pl.program_id(0) + first_tile) * tm


def _adaln(x, g, is_ctx, sh_c, sh_b, sc_c, sc_b):
    r = lax.rsqrt(jnp.mean(x * x, axis=-1, keepdims=True) + RMS_EPS)
    gain = jnp.where(is_ctx, g * (1.0 + sc_c[0]), g * (1.0 + sc_b[0]))
    return (x * r) * gain + jnp.where(is_ctx, sh_c[0], sh_b[0])


def _mod_specs(nb, base, comps):
    specs = []
    for comp in comps:
        specs.append(pl.BlockSpec((1, 1, D_MODEL), lambda i, b, comp=comp: (base + nb * 3 + comp, 0, 0)))
        specs.append(pl.BlockSpec((1, 1, D_MODEL), lambda i, b, comp=comp: (base + b * 3 + comp, 0, 0)))
    return specs


def _row_spec(table, row):
    return pl.BlockSpec((1, 1, table.shape[-1]), lambda i, b: (row, 0, 0))


def _ffn_rows(x, is_ctx, sh_c, sh_b, sc_c, sc_b, gt_c, gt_b, g_ref, w1_ref, w2_ref):
    h = _adaln(x, g_ref[0], is_ctx, sh_c, sh_b, sc_c, sc_b).astype(BF16)
    gu = jnp.dot(h, w1_ref[...], preferred_element_type=F32)
    gate = gu[:, :D_FF]
    a = (gate * jax.nn.sigmoid(gate) * gu[:, D_FF:]).astype(BF16)
    y = jnp.dot(a, w2_ref[...], preferred_element_type=F32)
    return x + jnp.where(is_ctx, 0.5 * gt_c[0], 0.5 * gt_b[0]) * y


def _layer_weight(stack, index):
    lead = len(index)
    return pl.BlockSpec((None,) * lead + stack.shape[lead:], lambda i, b: tuple(index) + (0, 0),
                        pipeline_mode=pl.Buffered(1))


def _rope(x, cos, sin, first_half):
    w = x.shape[-1]
    nxt = pltpu.roll(x, w - ROPE_PAIRS, axis=1)
    prv = pltpu.roll(x, ROPE_PAIRS, axis=1)
    return x * cos + jnp.where(first_half, nxt, prv) * sin


def _proj_rows(x, is_ctx, sh_c, sh_b, sc_c, sc_b, g_ref, w_ref, cos_ref, sin_ref,
               pa_ref, pq_ref, pqs_ref, pk_ref, pv_ref, pu_ref):
    h = _adaln(x, g_ref[0], is_ctx, sh_c, sh_b, sc_c, sc_b).astype(BF16)
    p = jnp.dot(h, w_ref[...], preferred_element_type=F32)
    o = A_IN
    pa_ref[...] = p[:, :o]
    cos = cos_ref[...]
    sin = sin_ref[...]
    lane = lax.broadcasted_iota(jnp.int32, (1, B_WIDTH), 1)
    first_half = (lane % (2 * ROPE_PAIRS)) < ROPE_PAIRS
    cos_q = jnp.concatenate([cos] * (B_WIDTH // V7X_LANES), axis=1)
    sin_q = jnp.concatenate([sin] * (B_WIDTH // V7X_LANES), axis=1)
    q = _rope(p[:, o:o + B_WIDTH], cos_q, sin_q, first_half)
    q = q * (B_HEAD_DIM ** -0.5 * LOG2E)
    pq_ref[...] = q.astype(BF16)
    low_head = (lane % V7X_LANES) < B_HEAD_DIM
    q_sw = jnp.where(low_head, pltpu.roll(q, B_WIDTH - B_HEAD_DIM, axis=1), pltpu.roll(q, B_HEAD_DIM, axis=1))
    pqs_ref[...] = q_sw.astype(BF16)
    o += B_WIDTH
    k = _rope(p[:, o:o + B_KV_WIDTH], cos, sin, first_half[:, :B_KV_WIDTH])
    pk_ref[...] = k.astype(BF16)
    o += B_KV_WIDTH
    pv_ref[...] = p[:, o:o + B_KV_WIDTH].astype(BF16)
    o += B_KV_WIDTH
    pu_ref[...] = p[:, o:o + C_WIDTH]


def _head_kernel(*refs, lc, entry):
    if entry:
        ctx_ref, lat_ref, *refs = refs
        x = jnp.where(pl.program_id(0) * ctx_ref.shape[1] < lc, ctx_ref[0], lat_ref[0])
    else:
        x_ref, *refs = refs
        x = x_ref[...]
    ffn_refs, proj_refs, (x_out, *proj_outs) = refs[:9], refs[9:17], refs[17:]
    is_ctx = _is_ctx_row(x.shape[0], lc)
    x = _ffn_rows(x, is_ctx, *ffn_refs)
    x_out[...] = x
    _proj_rows(x, is_ctx, *proj_refs, *proj_outs)


def _layer_head(X, mod, bases, gains, layer, w1s, w2s, w_in_s, cos_t, sin_t, nb, lc):
    d = D_MODEL
    entry = isinstance(X, tuple)
    tm = ROW_TILE if entry else TOKEN_TILE
    tile = lambda w: pl.BlockSpec((tm, w), lambda i, b: (i, b))
    if entry:
        ctx, lat = X
        nct = lc // tm
        tt = lc + lat.shape[1]
        srcs = [pl.BlockSpec((1, tm, d), lambda i, b: (b, jnp.minimum(i, nct - 1), 0)),
                pl.BlockSpec((1, tm, d), lambda i, b: (b, jnp.maximum(i - nct, 0), 0))]
        args, alias = (ctx, lat), {}
    else:
        tt = X.shape[0]
        srcs, args, alias = [tile(d)], (X,), {0: 0}
    widths = (d, A_IN, B_WIDTH, B_WIDTH, B_KV_WIDTH, B_KV_WIDTH, C_WIDTH)
    dtypes = (F32, F32, BF16, BF16, BF16, BF16, F32)
    table = pl.BlockSpec((tm, V7X_LANES), lambda i, b: (i, 0))
    return pl.pallas_call(
        functools.partial(_head_kernel, lc=lc, entry=entry),
        grid=(tt // tm, nb),
        in_specs=srcs + _mod_specs(nb, bases[0], (0, 1, 2)) + [
                 _row_spec(gains, layer * 3), _layer_weight(w1s, (layer, 0)), _layer_weight(w2s, (layer, 0))]
                 + _mod_specs(nb, bases[1], (0, 1)) + [
                 _row_spec(gains, layer * 3 + 1), _layer_weight(w_in_s, (layer,)), table, table],
        out_specs=[tile(w) for w in widths],
        out_shape=[jax.ShapeDtypeStruct((tt, nb * w), dt) for w, dt in zip(widths, dtypes)],
        input_output_aliases=alias,
        compiler_params=_cparams(2, V7X_VMEM_LIMIT),
        name="layer_head",
    )(*args, *([mod] * 6), gains, w1s, w2s, *([mod] * 4), gains, w_in_s, cos_t, sin_t)


def _hgrn_pairs(lo, hi, reverse):
    if hi - lo <= HGRN_SUB:
        return []
    mid = (lo + hi) // 2
    here = (lo, mid, mid - lo, mid) if reverse else (mid, lo, mid - lo, mid - 1)
    return [here] + _hgrn_pairs(lo, mid, reverse) + _hgrn_pairs(mid, hi, reverse)


def _hgrn_kernel(pa_ref, lb_ref, ng_ref, o_ref,
                 qs_s, kf_s, kb_s, ks_s, bf_s, bb_s, v_s, o_s, p_s, st_s, mask_s, ones_s, tri_s, *, lc):
    tt = pa_ref.shape[0]
    w = A_WIDTH
    ch = HGRN_CHUNK
    sub = HGRN_SUB
    n_sub = ch // sub
    blk = HGRN_PREP_ROWS
    n_blk = tt // blk
    n_chunks = tt // ch
    n_ctx_chunks = lc // ch
    nt = (((1,), (1,)), ((), ()))
    tn = (((0,), (0,)), ((), ()))

    row = lax.broadcasted_iota(jnp.int32, (w, w), 0)
    col = lax.broadcasted_iota(jnp.int32, (w, w), 1)
    same_head = jnp.where((row // A_HEAD_DIM) == (col // A_HEAD_DIM), 1.0, 0.0)
    mask_s[...] = same_head
    ones_s[...] = same_head.astype(BF16)
    lane = lax.broadcasted_iota(jnp.int32, (1, w), 1)
    head_mask = [jnp.where(lane // A_HEAD_DIM == h, 1.0, 0.0) for h in range(A_HEADS)]
    r = lax.broadcasted_iota(jnp.int32, (blk, blk), 0)
    c = lax.broadcasted_iota(jnp.int32, (blk, blk), 1)
    same_chunk = (r // ch) == (c // ch)
    tri_s[0] = jnp.where(same_chunk, jnp.where(c <= r, 1.0, 0.0), 0.0).astype(BF16)
    tri_s[1] = jnp.where(same_chunk, jnp.where(c >= r, 1.0, 0.0), 0.0).astype(BF16)
    st_s[...] = jnp.zeros(st_s.shape, F32)

    def chunk_cumsum(logf, tri):
        hi = logf.astype(BF16)
        rem = logf - hi.astype(F32)
        mid = rem.astype(BF16)
        lo = (rem - mid.astype(F32)).astype(BF16)
        return (jnp.dot(tri, hi, preferred_element_type=F32)
                + jnp.dot(tri, mid, preferred_element_type=F32)
                + jnp.dot(tri, lo, preferred_element_type=F32))

    def load(ref, rows):
        return jnp.concatenate([ref[c, rows, :] for c in range(w // V7X_LANES)], axis=1)

    def store(ref, rows, val):
        for c in range(w // V7X_LANES):
            ref[c, rows, :] = val[:, c * V7X_LANES:(c + 1) * V7X_LANES]

    def load_row(ref, r):
        return load(ref, pl.ds(r, sub, stride=0))

    def prep_body(i, carry):
        rows = pl.ds(pl.multiple_of(i * blk, blk), blk)
        q = pa_ref[rows, 0:w]
        qs_s[rows, :] = q * jax.nn.sigmoid(q)
        store(v_s, rows, pa_ref[rows, w:2 * w])
        o_s[rows, :] = jnp.zeros((blk, w), F32)
        k_sum = None
        for d, (k_s, b_s) in enumerate(((kf_s, bf_s), (kb_s, bb_s))):
            lb = lb_ref[d:d + 1, :]
            z = pa_ref[rows, (2 + d) * w:(3 + d) * w]
            t = jnp.exp(-jnp.abs(z))
            big = 1.0 / (1.0 + t)
            small = t * big
            f = lb + (1.0 - lb) * jnp.where(z >= 0, big, small)
            k = (1.0 - lb) * jnp.where(z >= 0, small, big)
            store(k_s, rows, k)
            k_sum = k if k_sum is None else k_sum + k
            store(b_s, rows, chunk_cumsum(jnp.log2(f), tri_s[d]))
        store(ks_s, rows, k_sum)
        return carry
    lax.fori_loop(0, n_blk, prep_body, 0, unroll=3)

    def expand(blocks):
        return jnp.concatenate([x * head_mask[h] for h in range(A_HEADS) for x in blocks], axis=0).astype(BF16)

    def expand_cached(masked, s0, n):
        j0 = s0 // sub
        return jnp.concatenate([masked[j][h] for h in range(A_HEADS) for j in range(j0, j0 + n // sub)],
                               axis=0).astype(BF16)

    def diag_scores(r0):
        t_i = lax.broadcasted_iota(jnp.int32, (sub, w), 0)
        for j in range(n_sub):
            base = r0 + j * sub
            rows = pl.ds(base, sub)
            q = qs_s[rows, :]
            bf = load(bf_s, rows)
            bb = load(bb_s, rows)
            for s in range(sub):
                kf_row = load_row(kf_s, base + s)
                kb_row = load_row(kb_s, base + s)
                arg = jnp.where(t_i >= s, bf - load_row(bf_s, base + s), bb - load_row(bb_s, base + s))
                kk = jnp.where(t_i > s, kf_row, jnp.where(t_i < s, kb_row, load_row(ks_s, base + s)))
                p_s[(j * sub + s) * sub:(j * sub + s + 1) * sub, :] = q * jnp.exp2(arg) * kk
        return jnp.dot(p_s[...].astype(BF16), ones_s[...], preferred_element_type=F32)

    def diag_apply(r0, a):
        out = []
        for j in range(n_sub):
            acc = None
            for s in range(sub):
                v_row = load_row(v_s, r0 + j * sub + s)
                term = a[(j * sub + s) * sub:(j * sub + s + 1) * sub, :] * v_row
                acc = term if acc is None else acc + term
            out.append(acc)
        return out

    def dir_scores(r0, d):
        k_s, b_s = (kf_s, bf_s) if d == 0 else (kb_s, bb_s)
        rows8 = lambda off: pl.ds(r0 + off, sub)
        q_blk = lambda off: qs_s[rows8(off), :]
        b_blk = lambda off: load(b_s, rows8(off))
        k_blk = lambda off: load(k_s, rows8(off))
        offs = lambda start, n: range(start, start + n, sub)
        b_last = load_row(b_s, r0 + ((ch - 1) if d == 0 else 0))
        pairs = _hgrn_pairs(0, ch, d == 1)
        scores = []
        for t0, s0, n, ref in pairs:
            b_ref = load_row(b_s, r0 + ref)
            qx = jnp.concatenate([q_blk(o) * jnp.exp2(b_blk(o) - b_ref) for o in offs(t0, n)], axis=0).astype(BF16)
            kx = expand([k_blk(o) * jnp.exp2(b_ref - b_blk(o)) for o in offs(s0, n)])
            scores.append(lax.dot_general(qx, kx, nt, preferred_element_type=F32))
        st = st_s[d]
        qe = jnp.concatenate([q_blk(o) * jnp.exp2(b_blk(o)) for o in offs(0, ch)], axis=0).astype(BF16)
        kd = jnp.concatenate([k_blk(o) * jnp.exp2(b_last - b_blk(o)) for o in offs(0, ch)], axis=0).astype(BF16)
        o = lax.dot_general(qe, st.astype(BF16), nt, preferred_element_type=F32)
        ut = lax.dot_general(load(v_s, pl.ds(r0, ch)).astype(BF16), kd, tn, preferred_element_type=F32)
        return pairs, scores, o, ut, st, b_last[0:1, :]

    def masked_values(r0):
        blocks = [load(v_s, pl.ds(r0 + j * sub, sub)) for j in range(n_sub)]
        return [[v * head_mask[h] for h in range(A_HEADS)] for v in blocks]

    def dir_values(pairs, scores, v_masked):
        return [jnp.dot(sc.astype(BF16), expand_cached(v_masked, s0, n), preferred_element_type=F32)
                for (t0, s0, n, ref), sc in zip(pairs, scores)]

    def dir_finish(d, pairs, contribs, o, ut, st, b_last):
        st_s[d] = st * jnp.exp2(b_last) + ut * mask_s[...]
        out = [o[j * sub:(j + 1) * sub, :] for j in range(n_sub)]
        for (t0, s0, n, ref), contrib in zip(pairs, contribs):
            for jj in range(n // sub):
                out[t0 // sub + jj] = out[t0 // sub + jj] + contrib[jj * sub:(jj + 1) * sub, :]
        return out

    def chunk_body(i, carry):
        cb = jnp.where(i < n_ctx_chunks, n_ctx_chunks - 1 - i, n_chunks + n_ctx_chunks - 1 - i)
        rf = pl.multiple_of(i * ch, ch)
        rb = pl.multiple_of(cb * ch, ch)
        pf, scf, of, utf, stf, blf = dir_scores(rf, 0)
        pb, scb, ob, utb, stb, blb = dir_scores(rb, 1)
        a = diag_scores(rf)
        cf = dir_values(pf, scf, masked_values(rf))
        cbw = dir_values(pb, scb, masked_values(rb))
        o_fwd = dir_finish(0, pf, cf, of, utf, stf, blf)
        o_bwd = dir_finish(1, pb, cbw, ob, utb, stb, blb)
        o_diag = diag_apply(rf, a)
        rows_f = pl.ds(rf, ch)
        o_s[rows_f, :] = o_s[rows_f, :] + jnp.concatenate([x + y for x, y in zip(o_diag, o_fwd)], axis=0)
        rows_b = pl.ds(rb, ch)
        o_s[rows_b, :] = o_s[rows_b, :] + jnp.concatenate(o_bwd, axis=0)
        return carry
    lax.fori_loop(0, n_chunks, chunk_body, 0, unroll=3)

    def out_body(i, carry):
        rows = pl.ds(pl.multiple_of(i * blk, blk), blk)
        o = o_s[rows, :]
        hi, lo = _split_hi_lo(o * o)
        ones = ones_s[...]
        ms = (jnp.dot(hi, ones, preferred_element_type=F32)
              + jnp.dot(lo, ones, preferred_element_type=F32)) * (1.0 / A_HEAD_DIM)
        g = pa_ref[rows, 4 * w:5 * w]
        o_ref[rows, :] = (o * lax.rsqrt(ms + RMS_EPS) * ng_ref[...] * (g * jax.nn.sigmoid(g))).astype(o_ref.dtype)
        return carry
    lax.fori_loop(0, n_blk, out_body, 0, unroll=3)


def _hgrn(pa, lower_bound, norm_g, nb, lc):
    tt = pa.shape[0]
    w = A_WIDTH
    seq = lambda: pltpu.VMEM((tt, w), F32)
    split = lambda: pltpu.VMEM((w // V7X_LANES, tt, V7X_LANES), F32)
    return pl.pallas_call(
        functools.partial(_hgrn_kernel, lc=lc),
        grid=(nb,),
        in_specs=[pl.BlockSpec((tt, A_IN), lambda b: (0, b)),
                  pl.BlockSpec((2, w), lambda b: (0, 0)),
                  pl.BlockSpec((1, w), lambda b: (0, 0))],
        out_specs=pl.BlockSpec((tt, w), lambda b: (0, b)),
        out_shape=jax.ShapeDtypeStruct((tt, nb * w), BF16),
        scratch_shapes=[seq(), split(), split(), split(), split(), split(), split(), seq(),
                        pltpu.VMEM((HGRN_CHUNK * HGRN_SUB, w), F32),
                        pltpu.VMEM((2, w, w), F32), pltpu.VMEM((w, w), F32), pltpu.VMEM((w, w), BF16),
                        pltpu.VMEM((2, HGRN_PREP_ROWS, HGRN_PREP_ROWS), BF16)],
        compiler_params=_cparams(1, V7X_VMEM_LIMIT),
        name="hgrn2_mixer",
    )(pa, lower_bound, norm_g.reshape(1, w))


def _attn_kernel(sink_ref, q_ref, qs_ref, k_ref, v_ref, o_ref, kz_s, *, lc):
    tt = k_ref.shape[0]
    blk = ATTN_BLOCK
    hd = B_HEAD_DIM
    j = pl.program_id(1)
    n_ctx_blk = lc // blk
    n_blk = tt // blk
    n_sub = q_ref.shape[0] // blk
    gw = B_GROUP * blk
    nt = (((1,), (1,)), ((), ()))
    tn = (((0,), (0,)), ((), ()))

    @pl.when(j == 0)
    def _():
        lane = lax.broadcasted_iota(jnp.int32, (1, B_KV_WIDTH), 1)
        for g in range(B_KV_HEADS):
            keep = (lane // hd) == g

            def body(i, carry):
                rows = pl.ds(pl.multiple_of(i * blk, blk), blk)
                k = k_ref[rows, :]
                kz_s[g, rows, :] = jnp.where(keep, k, jnp.zeros_like(k))
                return carry
            lax.fori_loop(0, n_blk, body, 0)

    def q_stack(g, qi):
        parts = []
        for hh in range(B_GROUP):
            h = g * B_GROUP + hh
            src = q_ref if (h % 2) == g else qs_ref
            parts.append(src[qi * blk:(qi + 1) * blk, (h // 2) * V7X_LANES:(h // 2 + 1) * V7X_LANES])
        return jnp.concatenate(parts, axis=0)

    def scores(g, qi, rows):
        return lax.dot_general(kz_s[g, rows, :], q_stack(g, qi), nt, preferred_element_type=F32)

    def softmax_values(g, pieces):
        sink = jnp.concatenate([jnp.full((1, blk), sink_ref[g * B_GROUP + hh] * LOG2E, F32)
                                for hh in range(B_GROUP)], axis=1)
        m = sink
        for s, _ in pieces:
            m = jnp.maximum(m, jnp.max(s, axis=0, keepdims=True))
        den = jnp.exp2(sink - m)
        probs = []
        for s, _ in pieces:
            p = jnp.exp2(s - m)
            den = den + jnp.sum(p, axis=0, keepdims=True)
            probs.append(p.astype(BF16))
        values = jnp.concatenate([vv for _, vv in pieces], axis=0)
        acc = lax.dot_general(values, jnp.concatenate(probs, axis=0), tn,
                              preferred_element_type=F32)
        return acc[g * hd:(g + 1) * hd, :] * (1.0 / den)

    def write(qi, outs):
        for pair in range(B_Q_HEADS // 2):
            g, hh = (2 * pair) // B_GROUP, (2 * pair) % B_GROUP
            both = jnp.concatenate([outs[g][:, hh * blk:(hh + 1) * blk],
                                    outs[g][:, (hh + 1) * blk:(hh + 2) * blk]], axis=0)
            o_ref[qi * blk:(qi + 1) * blk, pair * V7X_LANES:(pair + 1) * V7X_LANES] = both.T.astype(o_ref.dtype)

    def ctx_scores(qi):
        return [[scores(g, qi, pl.ds(0, lc))] for g in range(B_KV_HEADS)]

    def ctx_pieces(qi, sc):
        return [[(sc[g][0], v_ref[0:lc, :])] for g in range(B_KV_HEADS)]

    def band_rows(jb):
        prev = pl.ds(pl.multiple_of((jb - 1) * blk, blk), blk)
        own = pl.ds(pl.multiple_of(jb * blk, blk), blk)
        nxt = pl.ds(pl.multiple_of(jnp.minimum(jb + 1, n_blk - 1) * blk, blk), blk)
        return prev, own, nxt, pl.ds(0, lc)

    def band_scores(qi, jb):
        return [[scores(g, qi, r) for r in band_rows(jb)] for g in range(B_KV_HEADS)]

    def band_pieces(jb, sc):
        key = lax.broadcasted_iota(jnp.int32, (blk, gw), 0)
        qry = lax.broadcasted_iota(jnp.int32, (blk, gw), 1) % blk
        keep_prev = key >= qry + jnp.where(jb > n_ctx_blk, 0, blk)
        keep_next = key <= qry - jnp.where(jb < n_blk - 1, 0, blk)
        masked = MASK_VALUE * LOG2E
        r_prev, r_own, r_next, r_ctx = band_rows(jb)
        out = []
        for g in range(B_KV_HEADS):
            sp, so, sn, sx = sc[g]
            out.append([(jnp.where(keep_prev, sp, masked), v_ref[r_prev, :]), (so, v_ref[r_own, :]),
                        (jnp.where(keep_next, sn, masked), v_ref[r_next, :]), (sx, v_ref[r_ctx, :])])
        return out

    first = j * n_sub

    @pl.when(first < n_ctx_blk)
    def _():
        sc = [ctx_scores(qi) for qi in range(n_sub)]
        for qi in range(n_sub):
            pieces = ctx_pieces(qi, sc[qi])
            write(qi, [softmax_values(g, pieces[g]) for g in range(B_KV_HEADS)])

    @pl.when(first >= n_ctx_blk)
    def _():
        sc = [band_scores(qi, first + qi) for qi in range(n_sub)]
        for qi in range(n_sub):
            pieces = band_pieces(first + qi, sc[qi])
            write(qi, [softmax_values(g, pieces[g]) for g in range(B_KV_HEADS)])


def _attn(pq, pqs, pk, pv, sink, nb, lc):
    tt = pq.shape[0]
    blk = ATTN_Q_BLOCKS * ATTN_BLOCK
    assert WINDOW == ATTN_BLOCK and lc % blk == 0 and tt % blk == 0
    return pl.pallas_call(
        functools.partial(_attn_kernel, lc=lc),
        grid=(nb, tt // blk),
        in_specs=[pl.BlockSpec(memory_space=pltpu.SMEM),
                  pl.BlockSpec((blk, B_WIDTH), lambda b, j: (j, b)),
                  pl.BlockSpec((blk, B_WIDTH), lambda b, j: (j, b)),
                  pl.BlockSpec((tt, B_KV_WIDTH), lambda b, j: (0, b)),
                  pl.BlockSpec((tt, B_KV_WIDTH), lambda b, j: (0, b))],
        out_specs=pl.BlockSpec((blk, B_WIDTH), lambda b, j: (j, b)),
        out_shape=jax.ShapeDtypeStruct((tt, nb * B_WIDTH), BF16),
        scratch_shapes=[pltpu.VMEM((B_KV_HEADS, tt, B_KV_WIDTH), BF16)],
        compiler_params=_cparams(2),
        name="window_gqa",
    )(sink, pq, pqs, pk, pv)


def _s5_tile_index(reverse, i, n_ctx, n_all):
    if not reverse:
        return i
    return jnp.where(i < n_ctx, n_ctx - 1 - i, n_all + n_ctx - 1 - i)


def _s5_kernel(u_ref, a_ref, wb_ref, wc_ref, y_ref, u_s, y_s, h_s, *, nb, reverse):
    i = pl.program_id(0)
    n = C_NSTATE
    steps = u_ref.shape[0]
    halves = C_WIDTH // V7X_LANES
    group = S5_GROUP
    n_groups = steps // group

    @pl.when(i == 0)
    def _():
        h_s[...] = jnp.zeros(h_s.shape, F32)

    for b in range(nb):
        for c in range(halves):
            lanes = slice(b * C_WIDTH + c * V7X_LANES, b * C_WIDTH + (c + 1) * V7X_LANES)
            u_s[c, pl.ds(b, steps, stride=nb), :] = u_ref[:, lanes]
    a_re = jnp.broadcast_to(a_ref[0:1, :], (nb, n))
    a_im = jnp.broadcast_to(a_ref[1:2, :], (nb, n))

    def drive(g):
        rows = slice(g * group * nb, (g + 1) * group * nb)
        u = jnp.concatenate([u_s[c, rows, :] for c in range(halves)], axis=1)
        return jnp.dot(u.astype(BF16), wb_ref[...], preferred_element_type=F32)

    order = list(range(n_groups))[::-1] if reverse else list(range(n_groups))
    lookahead = 2
    driven = {g: drive(g) for g in order[:lookahead]}
    h_re, h_im = h_s[:, 0:n], h_s[:, n:2 * n]
    for pos, g in enumerate(order):
        bu = driven.pop(g)
        states = [None] * group
        for t in (range(group - 1, -1, -1) if reverse else range(group)):
            slab = bu[t * nb:(t + 1) * nb, :]
            h_re, h_im = (a_re * h_re - a_im * h_im + slab[:, 0:n],
                          a_re * h_im + a_im * h_re + slab[:, n:2 * n])
            states[t] = jnp.concatenate([h_re, h_im], axis=1)
        if pos + lookahead < n_groups:
            nxt = order[pos + lookahead]
            driven[nxt] = drive(nxt)
        hb = jnp.concatenate(states, axis=0).astype(BF16)
        y = jnp.dot(hb, wc_ref[...], preferred_element_type=F32)
        rows = slice(g * group * nb, (g + 1) * group * nb)
        for c in range(halves):
            y_s[c, rows, :] = y[:, c * V7X_LANES:(c + 1) * V7X_LANES]
    h_s[:, 0:n] = h_re
    h_s[:, n:2 * n] = h_im
    for b in range(nb):
        for c in range(halves):
            lanes = slice(b * C_WIDTH + c * V7X_LANES, b * C_WIDTH + (c + 1) * V7X_LANES)
            y_ref[:, lanes] = y_s[c, pl.ds(b, steps, stride=nb), :]


def _s5_scan(pu, a, wb, wc, index, nb, lc, reverse):
    tt = pu.shape[0]
    ts = S5_TILE
    n_all = tt // ts
    n_ctx = lc // ts
    tile = lambda i: (_s5_tile_index(reverse, i, n_ctx, n_all), 0)
    split = lambda: pltpu.VMEM((C_WIDTH // V7X_LANES, ts * nb, V7X_LANES), F32)
    const = lambda stack: pl.BlockSpec((None, None) + stack.shape[2:], lambda i: tuple(index) + (0, 0))
    return pl.pallas_call(
        functools.partial(_s5_kernel, nb=nb, reverse=reverse),
        grid=(n_all,),
        in_specs=[pl.BlockSpec((ts, nb * C_WIDTH), tile), const(a), const(wb), const(wc)],
        out_specs=pl.BlockSpec((ts, nb * C_WIDTH), tile),
        out_shape=jax.ShapeDtypeStruct((tt, nb * C_WIDTH), F32),
        scratch_shapes=[split(), split(), pltpu.VMEM((nb, 2 * C_NSTATE), F32)],
        compiler_params=_cparams(1, V7X_VMEM_LIMIT),
        name="s5_scan",
    )(pu, a, wb, wc)


def _s5_params(a_re, a_im, log_dt, b_re, b_im, c_re, c_im):
    eye = jnp.eye(C_GROUPS, dtype=F32)
    dt = jnp.exp(log_dt)[..., None]
    mag = jnp.exp(a_re * dt)
    ang = a_im * dt
    abar_re, abar_im = mag * jnp.cos(ang), mag * jnp.sin(ang)
    den = a_re * a_re + a_im * a_im
    coef_re = ((abar_re - 1.0) * a_re + abar_im * a_im) / den
    coef_im = (abar_im * a_re - (abar_re - 1.0) * a_im) / den
    b_re, b_im = b_re[:, None], b_im[:, None]
    bbar_re = coef_re[..., None] * b_re - coef_im[..., None] * b_im
    bbar_im = coef_re[..., None] * b_im + coef_im[..., None] * b_re
    lead = a_re.shape[:2]
    a = jnp.stack([abar_re.reshape(lead + (C_NSTATE,)), abar_im.reshape(lead + (C_NSTATE,))], axis=2)
    drive = lambda bb: jnp.einsum('lkgpc,gh->lkgchp', bb, eye).reshape(lead + (C_WIDTH, C_NSTATE))
    read = lambda cc: jnp.einsum('lkgcp,gh->lkgphc', cc, eye).reshape(lead + (C_NSTATE, C_WIDTH))
    wb = jnp.concatenate([drive(bbar_re), drive(bbar_im)], axis=-1)
    wc = jnp.concatenate([read(c_re), -read(c_im)], axis=-2)
    return a, wb.astype(BF16), wc.astype(BF16)


def _out_rows(x, is_ctx, gt_c, gt_b, a_ref, b_ref, u_ref, yf_ref, yb_ref, d_ref, gw_ref, gb_ref, wo_ref):
    y = d_ref[...] * u_ref[...] + yf_ref[...] + yb_ref[...]
    gelu = 0.5 * y * (1.0 + jnp.tanh(math.sqrt(2.0 / math.pi) * (y + 0.044715 * (y * y * y))))
    hg = jnp.dot(gelu.astype(BF16), gw_ref[...], preferred_element_type=F32) + gb_ref[...]
    c = hg[:, :C_WIDTH] * jax.nn.sigmoid(hg[:, C_WIDTH:])
    mix = (jnp.dot(a_ref[...], wo_ref[0:A_WIDTH, :], preferred_element_type=F32)
           + jnp.dot(b_ref[...], wo_ref[A_WIDTH:A_WIDTH + B_WIDTH, :], preferred_element_type=F32)
           + jnp.dot(c.astype(BF16), wo_ref[A_WIDTH + B_WIDTH:, :], preferred_element_type=F32))
    return x + jnp.where(is_ctx, gt_c[0], gt_b[0]) * mix


def _tail_kernel(x_ref, *refs, lc, final):
    x = x_ref[...]
    tm = x.shape[0]
    is_ctx = _is_ctx_row(tm, lc, lc // tm if final else 0)
    x = _out_rows(x, is_ctx, *refs[:11])
    y = _ffn_rows(x, is_ctx, *refs[11:20])
    if final:
        fg_ref, o_ref = refs[20:]
        o_ref[0] = y * lax.rsqrt(jnp.mean(y * y, axis=-1, keepdims=True) + RMS_EPS) * fg_ref[...]
    else:
        refs[20][...] = y


def _layer_tail(X, mod, bases, gains, layer, a_o, b_o, pu, y2, s5_d, glu_w_s, glu_b, w_out_s, w1s, w2s, nb, lc,
                final_g=None):
    tt = X.shape[0]
    d = D_MODEL
    final = final_g is not None
    tm = ROW_TILE if final else TOKEN_TILE
    first = lc // tm if final else 0
    const = lambda shape: pl.BlockSpec(shape, lambda i, b: (0,) * len(shape))
    tile = lambda w: pl.BlockSpec((tm, w), lambda i, b: (i + first, b))
    extra_specs, extra, alias = [], (), {0: 0}
    out_spec, out_shape = tile(d), (tt, nb * d)
    if final:
        extra_specs, extra, alias = [const((1, d))], (final_g.reshape(1, d),), {}
        out_spec, out_shape = pl.BlockSpec((1, tm, d), lambda i, b: (b, i, 0)), (nb, tt - lc, d)
    return pl.pallas_call(
        functools.partial(_tail_kernel, lc=lc, final=final),
        grid=(tt // tm - first, nb),
        in_specs=[tile(d)] + _mod_specs(nb, bases[1], (2,)) + [
                 tile(A_WIDTH), tile(B_WIDTH), tile(C_WIDTH), tile(C_WIDTH), tile(C_WIDTH),
                 const((1, C_WIDTH)), _layer_weight(glu_w_s, (layer,)), const((1, 2 * C_WIDTH)),
                 _layer_weight(w_out_s, (layer,))]
                 + _mod_specs(nb, bases[2], (0, 1, 2)) + [
                 _row_spec(gains, layer * 3 + 2), _layer_weight(w1s, (layer, 1)), _layer_weight(w2s, (layer, 1))]
                 + extra_specs,
        out_specs=out_spec,
        out_shape=jax.ShapeDtypeStruct(out_shape, F32),
        input_output_aliases=alias,
        compiler_params=_cparams(2, V7X_VMEM_LIMIT),
        name="layer_tail",
    )(X, mod, mod, a_o, b_o, pu, y2[0], y2[1], s5_d.reshape(1, C_WIDTH), glu_w_s, glu_b.reshape(1, 2 * C_WIDTH),
      w_out_s, *([mod] * 6), gains, w1s, w2s, *extra)


def _rope_tables(length, lc):
    t = jnp.arange(length)
    pos = jnp.stack([(t // GRID_W).astype(F32), (t % GRID_W).astype(F32)], axis=1)
    inv_freq = ROPE_BASE ** (-jnp.arange(ROPE_PAIRS, dtype=F32) / ROPE_PAIRS)
    ang = pos[:, :, None] * inv_freq
    cos = jnp.cos(ang)[:, :, None, :]
    sin = jnp.sin(ang)[:, :, None, :]
    cos = jnp.broadcast_to(cos, (length, 2, 2, ROPE_PAIRS)).reshape(length, B_HEAD_DIM)
    sin = jnp.concatenate([-sin, sin], axis=2).reshape(length, B_HEAD_DIM)
    cos = jnp.concatenate([jnp.ones((lc, B_HEAD_DIM), F32), cos], axis=0)
    sin = jnp.concatenate([jnp.zeros((lc, B_HEAD_DIM), F32), sin], axis=0)
    return jnp.tile(cos, (1, V7X_LANES // B_HEAD_DIM)), jnp.tile(sin, (1, V7X_LANES // B_HEAD_DIM))


def kernel(x, c, ctx, c_ctx, ada_w, ada_b, norm_g, ffn_w1, ffn_w2, w_in, w_out, hgrn_lower_bounds,
           hgrn_norm_g, attn_sink, s5_a_re, s5_a_im, s5_log_dt, s5_b_re, s5_b_im, s5_c_re, s5_c_im,
           s5_d, s5_glu_w, s5_glu_b, final_norm_g):
    nb, length, d = x.shape
    lc = ctx.shape[1]
    tt = lc + length
    depth = ada_w.shape[0]
    assert d == D_MODEL and nb == 8 and lc % ROW_TILE == 0 and length % ROW_TILE == 0 and tt % TOKEN_TILE == 0

    act = jnp.concatenate([c, c_ctx[None], jnp.zeros((MOD_ROWS - nb - 1, d), F32)], axis=0)
    mod = _modulation(act, ada_w, ada_b).reshape(depth, MOD_ROWS, 3, 3, d)[:, :nb + 1]
    mod = mod.transpose(0, 2, 1, 3, 4).reshape(depth * 3 * (nb + 1) * 3, 1, d)
    cos_t, sin_t = _rope_tables(length, lc)
    lb_soft = jax.nn.softmax(hgrn_lower_bounds.astype(F32), axis=0)
    lower_bound = jnp.cumsum(lb_soft, axis=0) - lb_soft[0]

    w1s, w2s = ffn_w1.astype(BF16), ffn_w2.astype(BF16)
    w_in_s, w_out_s, glu_w_s = w_in.astype(BF16), w_out.astype(BF16), s5_glu_w.astype(BF16)
    s5a, s5wb, s5wc = _s5_params(s5_a_re, s5_a_im, s5_log_dt, s5_b_re, s5_b_im, s5_c_re, s5_c_im)
    gains = norm_g.reshape(depth * 3, 1, d)
    X = (ctx, x)
    for l in range(depth):
        base = [(l * 3 + sub) * (nb + 1) * 3 for sub in range(3)]
        X, pa, pq, pqs, pk, pv, pu = _layer_head(X, mod, base, gains, l, w1s, w2s, w_in_s, cos_t, sin_t, nb, lc)
        a_o = _hgrn(pa, lower_bound[l], hgrn_norm_g[l], nb, lc)
        b_o = _attn(pq, pqs, pk, pv, attn_sink[l], nb, lc)
        y2 = [_s5_scan(pu, s5a, s5wb, s5wc, (l, k), nb, lc, k == 1) for k in range(2)]
        X = _layer_tail(X, mod, base, gains, l, a_o, b_o, pu, y2, s5_d[l], glu_w_s, s5_glu_b[l], w_out_s, w1s, w2s,
                        nb, lc, final_g=final_norm_g if l == depth - 1 else None)
    return X
```
